```python
import jax, jax.numpy as jnp
from jax import lax
import numpy as np

D_MODEL = 1024
BATCH = 2
SEQ = 8192
DEPTH = 1
DEC_BATCH = 32
DEC_SEQ = 8
PAST_LEN = 16384
PAGE_SIZE = 128

ATTN_HEADS = 8
HEAD_DIM = 64
ATTN_WIDTH = ATTN_HEADS * HEAD_DIM
N_KV_HEADS = 2
KV_GROUP = ATTN_HEADS // N_KV_HEADS
IDX_HEADS = 8
IDX_DIM = 64
INDEX_TOPK = 256
ROPE_THETA = 500000.0
Q_BLOCK = 128
CONV_CH = D_MODEL - ATTN_WIDTH
CONV_WIDTH = 31
D_FF = 4 * D_MODEL
RMS_EPS = 1e-6
LN_EPS = 1e-5
IN_SIZES = (ATTN_WIDTH, N_KV_HEADS * HEAD_DIM, N_KV_HEADS * HEAD_DIM,
            IDX_HEADS * IDX_DIM, IDX_DIM, IDX_HEADS, CONV_CH, CONV_CH)
D_IN = sum(IN_SIZES)

kernel_name = 'hymba_conformer_dsa_decoder_step'


def rmsnorm(x, g):
    xf = x.astype(jnp.float32)
    y = xf * lax.rsqrt(jnp.mean(xf * xf, axis=-1, keepdims=True) + RMS_EPS)
    return (y * g.astype(jnp.float32)).astype(x.dtype)


def layernorm(x, g, b):
    xf = x.astype(jnp.float32)
    mu = jnp.mean(xf, axis=-1, keepdims=True)
    var = jnp.mean(jnp.square(xf - mu), axis=-1, keepdims=True)
    y = (xf - mu) * lax.rsqrt(var + LN_EPS)
    return (y * g.astype(jnp.float32) + b.astype(jnp.float32)).astype(x.dtype)


def partial_rope(x, pos):
    rot = x.shape[-1] // 4
    half = rot // 2
    inv_freq = jnp.power(ROPE_THETA, -jnp.arange(half, dtype=jnp.float32) * (2.0 / rot))
    ang = pos.astype(jnp.float32)[:, None] * inv_freq[None, :]
    cos = jnp.cos(ang)[None, :, None, :]
    sin = jnp.sin(ang)[None, :, None, :]
    xf = x.astype(jnp.float32)
    x1 = xf[..., :half]
    x2 = xf[..., half:rot]
    out = jnp.concatenate([x1 * cos - x2 * sin, x2 * cos + x1 * sin, xf[..., rot:]], axis=-1)
    return out.astype(x.dtype)


def in_projection(x, g_pre, w_in, pos):
    B, T = x.shape[0], x.shape[1]
    h = rmsnorm(x, g_pre)
    z = jnp.einsum('btd,de->bte', h, w_in)
    splits = [int(s) for s in np.cumsum(IN_SIZES)[:-1]]
    q, k, v, qi, ki, wi, ua, ub = jnp.split(z, splits, axis=-1)
    q = partial_rope(q.reshape(B, T, ATTN_HEADS, HEAD_DIM), pos)
    k = partial_rope(k.reshape(B, T, N_KV_HEADS, HEAD_DIM), pos)
    v = v.reshape(B, T, N_KV_HEADS, HEAD_DIM)
    qi = partial_rope(qi.reshape(B, T, IDX_HEADS, IDX_DIM), pos)
    ki = partial_rope(ki.reshape(B, T, 1, IDX_DIM), pos)[:, :, 0]
    u = ua * jax.nn.sigmoid(ub)
    return q, k, v, qi, ki, wi, u


def index_topk(qi, wi, ki, q_pos, topk):
    s = jnp.einsum('bthd,bsd->bths', qi.astype(jnp.float32), ki.astype(jnp.float32)) * (IDX_DIM ** -0.5)
    score = jnp.einsum('bths,bth->bts', jax.nn.relu(s), wi.astype(jnp.float32))
    visible = jnp.arange(ki.shape[1])[None, :] <= q_pos[:, None]
    score = jnp.where(visible[None], score, -jnp.inf)
    _, idx = lax.top_k(score, topk)
    valid = idx <= q_pos[None, :, None]
    return idx, valid


def gather_rows(a, idx):
    return jax.vmap(lambda r, i: r[i])(a, idx)


def sparse_attend(q, k_sel, v_sel, valid):
    B, T = q.shape[0], q.shape[1]
    qg = q.reshape(B, T, N_KV_HEADS, KV_GROUP, HEAD_DIM).astype(jnp.float32)
    s = jnp.einsum('btkgd,btskd->btkgs', qg, k_sel.astype(jnp.float32)) * (HEAD_DIM ** -0.5)
    s = jnp.where(valid[:, :, None, None, :], s, -jnp.inf)
    p = jax.nn.softmax(s, axis=-1)
    o = jnp.einsum('btkgs,btskd->btkgd', p, v_sel.astype(jnp.float32))
    return o.reshape(B, T, ATTN_WIDTH).astype(q.dtype)


def prompt_attention(q, k, v, qi, wi, ki, topk):
    B = q.shape[0]
    n_blocks = SEQ // Q_BLOCK

    def block(i):
        start = i * Q_BLOCK
        qb = lax.dynamic_slice_in_dim(q, start, Q_BLOCK, axis=1)
        qib = lax.dynamic_slice_in_dim(qi, start, Q_BLOCK, axis=1)
        wib = lax.dynamic_slice_in_dim(wi, start, Q_BLOCK, axis=1)
        pos = start + jnp.arange(Q_BLOCK, dtype=jnp.int32)
        idx, valid = index_topk(qib, wib, ki, pos, topk)
        return sparse_attend(qb, gather_rows(k, idx), gather_rows(v, idx), valid)

    out = lax.map(block, jnp.arange(n_blocks, dtype=jnp.int32))
    return jnp.moveaxis(out, 0, 1).reshape(B, SEQ, ATTN_WIDTH)


def sample_attention(q, k, v, qi, wi, ki, pool_k, pool_v, pool_ik, page_table, topk):
    DB = q.shape[0]
    ki_past = pool_ik[page_table].reshape(DB, PAST_LEN, IDX_DIM)
    ki_all = jnp.concatenate([ki_past, ki], axis=1)
    pos = PAST_LEN + jnp.arange(DEC_SEQ, dtype=jnp.int32)
    idx, valid = index_topk(qi, wi, ki_all, pos, topk)
    is_past = idx < PAST_LEN
    ip = jnp.minimum(idx, PAST_LEN - 1)
    phys_page = jax.vmap(lambda pt, i: pt[i])(page_table, ip // PAGE_SIZE)
    phys = phys_page * PAGE_SIZE + ip % PAGE_SIZE
    k_past = pool_k.reshape(-1, N_KV_HEADS, HEAD_DIM)[phys]
    v_past = pool_v.reshape(-1, N_KV_HEADS, HEAD_DIM)[phys]
    inew = jnp.clip(idx - PAST_LEN, 0, DEC_SEQ - 1)
    sel = is_past[..., None, None]
    k_sel = jnp.where(sel, k_past, gather_rows(k, inew))
    v_sel = jnp.where(sel, v_past, gather_rows(v, inew))
    return sparse_attend(q, k_sel, v_sel, valid)


def conv_branch(u, hist, w_dw, b_dw, ln_g, ln_b):
    full = jnp.concatenate([hist, u], axis=1)
    c = lax.conv_general_dilated(full, w_dw[:, None, :], (1,), 'VALID',
                                 dimension_numbers=('NWC', 'WIO', 'NWC'),
                                 feature_group_count=CONV_CH) + b_dw
    c = layernorm(c, ln_g, ln_b)
    c = c * jax.nn.sigmoid(c)
    return c, full[:, -(CONV_WIDTH - 1):]


def mix_and_mlp(x, attn_o, conv_o, w_out, g_mix_post, g_mlp_pre, w_up, w_down, g_mlp_post):
    m = jnp.einsum('bte,ed->btd', jnp.concatenate([attn_o, conv_o], axis=-1), w_out)
    x = x + rmsnorm(m, g_mix_post)
    h = rmsnorm(x, g_mlp_pre)
    f = jnp.einsum('btf,fd->btd', jnp.square(jax.nn.relu(jnp.einsum('btd,df->btf', h, w_up))), w_down)
    return x + rmsnorm(f, g_mlp_post)


def setup_inputs(seed: int = 0) -> dict:
    key = jax.random.key(seed)
    ks = jax.random.split(key, 24)
    n_pages = PAST_LEN // PAGE_SIZE
    n_used = DEC_BATCH * n_pages
    n_pool = n_used + n_used // 4
    nrm = jax.random.normal
    f32 = jnp.float32
    page_table = jax.random.permutation(ks[0], n_pool)[:n_used].reshape(DEC_BATCH, n_pages).astype(jnp.int32)
    return {
        'x_prompt': nrm(ks[1], (BATCH, SEQ, D_MODEL), f32),
        'x_sample': nrm(ks[2], (DEC_BATCH, DEC_SEQ, D_MODEL), f32),
        'cache_k': nrm(ks[3], (DEPTH, n_pool, PAGE_SIZE, N_KV_HEADS, HEAD_DIM), f32),
        'cache_v': nrm(ks[4], (DEPTH, n_pool, PAGE_SIZE, N_KV_HEADS, HEAD_DIM), f32),
        'cache_idx_k': nrm(ks[5], (DEPTH, n_pool, PAGE_SIZE, IDX_DIM), f32),
        'state_conv': 0.5 * nrm(ks[6], (DEPTH, DEC_BATCH, CONV_WIDTH - 1, CONV_CH), f32),
        'page_table': page_table,
        'norm_mix_pre': 1.0 + 0.05 * nrm(ks[7], (DEPTH, D_MODEL), f32),
        'w_in': nrm(ks[8], (DEPTH, D_MODEL, D_IN), f32) * D_MODEL ** -0.5,
        'conv_dw_w': nrm(ks[9], (DEPTH, CONV_WIDTH, CONV_CH), f32) * CONV_WIDTH ** -0.5,
        'conv_dw_b': 0.02 * nrm(ks[10], (DEPTH, CONV_CH), f32),
        'conv_ln_g': 1.0 + 0.05 * nrm(ks[11], (DEPTH, CONV_CH), f32),
        'conv_ln_b': 0.02 * nrm(ks[12], (DEPTH, CONV_CH), f32),
        'w_out': nrm(ks[13], (DEPTH, ATTN_WIDTH + CONV_CH, D_MODEL), f32) * (ATTN_WIDTH + CONV_CH) ** -0.5,
        'norm_mix_post': 1.0 + 0.05 * nrm(ks[14], (DEPTH, D_MODEL), f32),
        'norm_mlp_pre': 1.0 + 0.05 * nrm(ks[15], (DEPTH, D_MODEL), f32),
        'w_up': nrm(ks[16], (DEPTH, D_MODEL, D_FF), f32) * D_MODEL ** -0.5,
        'w_down': nrm(ks[17], (DEPTH, D_FF, D_MODEL), f32) * D_FF ** -0.5,
        'norm_mlp_post': 1.0 + 0.05 * nrm(ks[18], (DEPTH, D_MODEL), f32),
    }


def reference(x_prompt, x_sample, cache_k, cache_v, cache_idx_k, state_conv, page_table,
              norm_mix_pre, w_in, conv_dw_w, conv_dw_b, conv_ln_g, conv_ln_b, w_out,
              norm_mix_post, norm_mlp_pre, w_up, w_down, norm_mlp_post):
    topk_prompt = min(INDEX_TOPK, SEQ // 4)
    topk_sample = min(INDEX_TOPK, (PAST_LEN + DEC_SEQ) // 4)
    pos_p = jnp.arange(SEQ, dtype=jnp.int32)
    pos_s = PAST_LEN + jnp.arange(DEC_SEQ, dtype=jnp.int32)
    xp, xs = x_prompt, x_sample
    kp_l, vp_l, ikp_l, cp_l = [], [], [], []
    ks_l, vs_l, iks_l, cs_l = [], [], [], []
    for l in range(DEPTH):
        q, k, v, qi, ki, wi, u = in_projection(xp, norm_mix_pre[l], w_in[l], pos_p)
        ao = prompt_attention(q, k, v, qi, wi, ki, topk_prompt)
        hist0 = jnp.zeros((xp.shape[0], CONV_WIDTH - 1, CONV_CH), u.dtype)
        co, conv_new = conv_branch(u, hist0, conv_dw_w[l], conv_dw_b[l], conv_ln_g[l], conv_ln_b[l])
        xp = mix_and_mlp(xp, ao, co, w_out[l], norm_mix_post[l], norm_mlp_pre[l],
                         w_up[l], w_down[l], norm_mlp_post[l])
        kp_l.append(k); vp_l.append(v); ikp_l.append(ki); cp_l.append(conv_new)
        q, k, v, qi, ki, wi, u = in_projection(xs, norm_mix_pre[l], w_in[l], pos_s)
        ao = sample_attention(q, k, v, qi, wi, ki, cache_k[l], cache_v[l], cache_idx_k[l],
                              page_table, topk_sample)
        co, conv_new = conv_branch(u, state_conv[l], conv_dw_w[l], conv_dw_b[l], conv_ln_g[l], conv_ln_b[l])
        xs = mix_and_mlp(xs, ao, co, w_out[l], norm_mix_post[l], norm_mlp_pre[l],
                         w_up[l], w_down[l], norm_mlp_post[l])
        ks_l.append(k); vs_l.append(v); iks_l.append(ki); cs_l.append(conv_new)
    new_k_prompt = jnp.stack(kp_l, 0)
    new_v_prompt = jnp.stack(vp_l, 0)
    new_idx_k_prompt = jnp.stack(ikp_l, 0)
    new_conv_prompt = jnp.stack(cp_l, 0)
    new_k_sample = jnp.stack(ks_l, 0)
    new_v_sample = jnp.stack(vs_l, 0)
    new_idx_k_sample = jnp.stack(iks_l, 0)
    new_conv_sample = jnp.stack(cs_l, 0)
    return (xp, xs, new_k_prompt, new_v_prompt, new_idx_k_prompt, new_conv_prompt,
            new_k_sample, new_v_sample, new_idx_k_sample, new_conv_sample)
```

```python
import functools

import numpy as np
import jax
import jax.numpy as jnp
from jax import lax
from jax.experimental import pallas as pl
from jax.experimental.pallas import tpu as pltpu

F32 = jnp.float32
BF16 = jnp.bfloat16
I32 = jnp.int32

ATTN_HEADS = 8
HEAD_DIM = 64
ATTN_WIDTH = ATTN_HEADS * HEAD_DIM
N_KV_HEADS = 2
KV_GROUP = ATTN_HEADS // N_KV_HEADS
IDX_HEADS = 8
IDX_DIM = 64
INDEX_TOPK = 256
ROPE_THETA = 500000.0
CONV_WIDTH = 31
RMS_EPS = 1e-6
LN_EPS = 1e-5
SCALE = 0.125
assert HEAD_DIM ** -0.5 == SCALE and IDX_DIM ** -0.5 == SCALE
assert N_KV_HEADS * HEAD_DIM == 128

LANES = 128
NEG = -1e30
HALO = 32
TOK_TILE = 256
Q_TILE = 256
K_BLOCK = 512
SEL_ROWS = 128
MLP_TILE = 256
FF_CHUNK = 1024
PAGES_PER_STEP = 16
VMEM_LIMIT = 56 * 1024 * 1024

KEY_NEG_INF = -2139095041
KEY_NAN_HI = 2139095041
INT_MAX = 2147483647


def _dot(a, b):
    return jnp.dot(a, b, preferred_element_type=F32)


def _dot_nt(a, b):
    return lax.dot_general(a, b, (((1,), (1,)), ((), ())), preferred_element_type=F32)


def _split(x):
    hi = x.astype(BF16)
    lo = (x - hi.astype(F32)).astype(BF16)
    return hi, lo


def _rms(x):
    return x * lax.rsqrt(jnp.mean(x * x, axis=-1, keepdims=True) + RMS_EPS)


def _sigmoid(x):
    return 1.0 / (1.0 + jnp.exp(-x))


def _inproj(x, g_pre, wm_ref, wih_ref, wil_ref, rc, ra, rb):
    h = _rms(x) * g_pre
    h_hi, h_lo = _split(h)
    zm = _dot(h_hi, wm_ref[...])
    wih = wih_ref[...]
    zi = _dot(h_hi, wih) + _dot(h_lo, wih) + _dot(h_hi, wil_ref[...])

    def rope(c):
        return c * rc + pltpu.roll(c, 8, 1) * ra + pltpu.roll(c, LANES - 8, 1) * rb

    nq = ATTN_HEADS * LANES
    q = [rope(zm[:, c * LANES:(c + 1) * LANES]) * SCALE for c in range(ATTN_HEADS)]
    k = rope(zm[:, nq:nq + LANES])
    v = zm[:, nq + LANES:nq + 2 * LANES]
    cch = zm.shape[1] - nq - 2 * LANES
    ua = zm[:, nq + 2 * LANES:nq + 2 * LANES + cch // 2]
    ub = zm[:, nq + 2 * LANES + cch // 2:]
    u = ua * _sigmoid(ub)
    qi = []
    for c in range(IDX_HEADS * IDX_DIM // LANES):
        r = rope(zi[:, c * LANES:(c + 1) * LANES]) * SCALE
        qi.append(r[:, :IDX_DIM])
        qi.append(r[:, IDX_DIM:])
    kc = zi[:, IDX_HEADS * IDX_DIM:IDX_HEADS * IDX_DIM + LANES]
    kic = rope(kc)
    wi = kc[:, IDX_DIM:IDX_DIM + IDX_HEADS]
    return dict(q=q, k=k, v=v, u=u, qi=qi, kic=kic, wi=wi)


def _conv_post(acc, lng, lnb):
    mu = jnp.mean(acc, axis=-1, keepdims=True)
    d = acc - mu
    var = jnp.mean(d * d, axis=-1, keepdims=True)
    y = d * lax.rsqrt(var + LN_EPS) * lng + lnb
    return y * _sigmoid(y)


def _inproj_prompt_kernel(x_ref, g_ref, wm_ref, wih_ref, wil_ref, rc_ref, ra_ref, rb_ref,
                          cw_ref, cb_ref, lng_ref, lnb_ref,
                          qpad_ref, kt32_ref, vt32_ref, ktb_ref, v0_ref, v1_ref, kit32_ref, kith_ref, kitl_ref,
                          qih_ref, qil_ref, wi_ref, co_ref, ctail_ref, ubuf):
    i = pl.program_id(1)
    tt = x_ref.shape[1]
    p = _inproj(x_ref[0], g_ref[...], wm_ref, wih_ref, wil_ref, rc_ref[...], ra_ref[...], rb_ref[...])
    for hh in range(ATTN_HEADS):
        qpad_ref[0, :, hh * LANES:(hh + 1) * LANES] = p["q"][hh].astype(BF16)
    kt = p["k"].T
    kt32_ref[0] = kt
    ktb_ref[0] = kt.astype(BF16)
    v = p["v"]
    vt32_ref[0] = v.T
    lane = lax.broadcasted_iota(I32, v.shape, 1)
    v0_ref[0] = jnp.where(lane < HEAD_DIM, v, 1.0).astype(BF16)
    v1_ref[0] = jnp.where(lane < HEAD_DIM, pltpu.roll(v, HEAD_DIM, 1), 1.0).astype(BF16)
    kit = p["kic"].T[:IDX_DIM]
    kit32_ref[0] = kit
    kith, kitl = _split(kit)
    kith_ref[0] = kith
    kitl_ref[0] = kitl
    wi_ref[0] = p["wi"]
    for hh in range(IDX_HEADS):
        hi, lo = _split(p["qi"][hh])
        qih_ref[0, hh] = hi
        qil_ref[0, hh] = lo

    @pl.when(i == 0)
    def _():
        ubuf[0:HALO, :] = jnp.zeros((HALO, ubuf.shape[1]), F32)

    @pl.when(i > 0)
    def _():
        ubuf[0:HALO, :] = ubuf[tt:tt + HALO, :]

    ubuf[HALO:HALO + tt, :] = p["u"]
    ctail_ref[0] = ubuf[tt:tt + HALO, :]

    rc_rows = 64
    cb = cb_ref[...]
    lng = lng_ref[...]
    lnb = lnb_ref[...]
    off = HALO - (CONV_WIDTH - 1)
    for r in range(tt // rc_rows):
        acc = jnp.broadcast_to(cb, (rc_rows, cb.shape[1]))
        for j in range(CONV_WIDTH):
            acc = acc + ubuf[pl.ds(off + j + r * rc_rows, rc_rows), :] * cw_ref[j:j + 1, :]
        co_ref[0, r * rc_rows:(r + 1) * rc_rows, :] = _conv_post(acc, lng, lnb).astype(BF16)


def _inproj_sample_kernel(x_ref, g_ref, wm_ref, wih_ref, wil_ref, rc_ref, ra_ref, rb_ref,
                          cw_ref, cb_ref, lng_ref, lnb_ref, st_ref,
                          qpad_ref, k32_ref, v32_ref, kb_ref, vb_ref, ki32_ref, kih_ref, kil_ref,
                          qih_ref, qil_ref, wi_ref, co_ref, cnew_ref, fbuf, uscr):
    hist, db, cch = st_ref.shape
    ds = x_ref.shape[0] // db
    p = _inproj(x_ref[...], g_ref[...], wm_ref, wih_ref, wil_ref, rc_ref[...], ra_ref[...], rb_ref[...])
    for hh in range(ATTN_HEADS):
        qpad_ref[:, hh * LANES:(hh + 1) * LANES] = p["q"][hh].astype(BF16)
    k32_ref[...] = p["k"]
    v32_ref[...] = p["v"]
    kb_ref[...] = p["k"].astype(BF16)
    vb_ref[...] = p["v"].astype(BF16)
    ki = p["kic"][:, :IDX_DIM]
    ki32_ref[...] = ki
    kih, kil = _split(ki)
    kih_ref[...] = kih
    kil_ref[...] = kil
    wi_ref[...] = p["wi"]
    for hh in range(IDX_HEADS):
        hi, lo = _split(p["qi"][hh])
        qih_ref[hh] = hi
        qil_ref[hh] = lo

    nch = cch // LANES
    u = p["u"]
    for cc in range(nch):
        uscr[cc] = u[:, cc * LANES:(cc + 1) * LANES]
    fbuf[0:hist] = st_ref[...]
    for t in range(ds):
        for cc in range(nch):
            fbuf[hist + t, :, cc * LANES:(cc + 1) * LANES] = uscr[cc, pl.ds(t, db, stride=ds), :]
    acc = jnp.broadcast_to(cb_ref[...].reshape(1, 1, cch), (ds, db, cch))
    for j in range(CONV_WIDTH):
        acc = acc + fbuf[j:j + ds] * cw_ref[j:j + 1, :].reshape(1, 1, cch)
    y = _conv_post(acc.reshape(ds * db, cch), lng_ref[...], lnb_ref[...])
    for t in range(ds):
        for cc in range(nch):
            uscr[cc, pl.ds(t, db, stride=ds), :] = y[t * db:(t + 1) * db, cc * LANES:(cc + 1) * LANES]
    for cc in range(nch):
        co_ref[:, cc * LANES:(cc + 1) * LANES] = uscr[cc].astype(BF16)
    cnew_ref[...] = fbuf[ds:ds + hist]


def _inproj_prompt(x, g_pre, wm, wih, wil, rc, ra, rb, cw, cb, lng, lnb):
    b, t, d = x.shape
    tt = TOK_TILE
    cch = cw.shape[1]
    full = lambda a: pl.BlockSpec(a.shape, lambda bi, i: (0,) * a.ndim)
    tok = lambda w: pl.BlockSpec((1, tt, w), lambda bi, i: (bi, i, 0))
    feat = lambda w: pl.BlockSpec((1, w, tt), lambda bi, i: (bi, 0, i))
    tab = pl.BlockSpec((tt, LANES), lambda bi, i: (i, 0))
    hm = pl.BlockSpec((1, IDX_HEADS, tt, IDX_DIM), lambda bi, i: (bi, 0, i, 0))
    sds = jax.ShapeDtypeStruct
    out_shape = [sds((b, t, ATTN_HEADS * LANES), BF16), sds((b, LANES, t), F32), sds((b, LANES, t), F32),
                 sds((b, LANES, t), BF16), sds((b, t, LANES), BF16), sds((b, t, LANES), BF16),
                 sds((b, IDX_DIM, t), F32), sds((b, IDX_DIM, t), BF16), sds((b, IDX_DIM, t), BF16),
                 sds((b, IDX_HEADS, t, IDX_DIM), BF16), sds((b, IDX_HEADS, t, IDX_DIM), BF16),
                 sds((b, t, IDX_HEADS), F32), sds((b, t, cch), BF16), sds((b, HALO, cch), F32)]
    out_specs = [tok(ATTN_HEADS * LANES), feat(LANES), feat(LANES), feat(LANES), tok(LANES), tok(LANES),
                 feat(IDX_DIM), feat(IDX_DIM), feat(IDX_DIM), hm, hm, tok(IDX_HEADS), tok(cch),
                 pl.BlockSpec((1, HALO, cch), lambda bi, i: (bi, 0, 0))]
    return pl.pallas_call(
        _inproj_prompt_kernel,
        grid=(b, t // tt),
        in_specs=[tok(d), full(g_pre), full(wm), full(wih), full(wil), tab, tab, tab,
                  full(cw), full(cb), full(lng), full(lnb)],
        out_specs=out_specs,
        out_shape=out_shape,
        scratch_shapes=[pltpu.VMEM((tt + HALO, cch), F32)],
        compiler_params=pltpu.CompilerParams(dimension_semantics=("arbitrary", "arbitrary"),
                                             vmem_limit_bytes=VMEM_LIMIT),
        name="inproj_prompt",
    )(x, g_pre, wm, wih, wil, rc, ra, rb, cw, cb, lng, lnb)


def _inproj_sample(x, g_pre, wm, wih, wil, rc, ra, rb, cw, cb, lng, lnb, state):
    n, d = x.shape
    hist, db, cch = state.shape
    ds = n // db
    sds = jax.ShapeDtypeStruct
    out_shape = [sds((n, ATTN_HEADS * LANES), BF16), sds((n, LANES), F32), sds((n, LANES), F32),
                 sds((n, LANES), BF16), sds((n, LANES), BF16),
                 sds((n, IDX_DIM), F32), sds((n, IDX_DIM), BF16), sds((n, IDX_DIM), BF16),
                 sds((IDX_HEADS, n, IDX_DIM), BF16), sds((IDX_HEADS, n, IDX_DIM), BF16),
                 sds((n, IDX_HEADS), F32), sds((n, cch), BF16), sds((hist, db, cch), F32)]
    return pl.pallas_call(
        _inproj_sample_kernel,
        out_shape=out_shape,
        scratch_shapes=[pltpu.VMEM((hist + ds, db, cch), F32), pltpu.VMEM((cch // LANES, n, LANES), F32)],
        compiler_params=pltpu.CompilerParams(vmem_limit_bytes=VMEM_LIMIT),
        name="inproj_sample",
    )(x, g_pre, wm, wih, wil, rc, ra, rb, cw, cb, lng, lnb, state)


def _key_to_f32(key):
    bits = jnp.where(key < 0, key ^ INT_MAX, key)
    return lax.bitcast_convert_type(bits, F32)


def _row_reduce(sources, rows, fn, op, init):
    c = jnp.full((rows, LANES), init, F32)
    for get, nblk, width, col0 in sources:
        def body(j, c, get=get, width=width, col0=col0):
            blk = get(j)
            col = col0 + j * width + lax.broadcasted_iota(I32, blk.shape, 1)
            val = fn(blk, col)
            for cc in range(width // LANES):
                c = op(c, val[:, cc * LANES:(cc + 1) * LANES])
            return c
        c = lax.fori_loop(0, nblk, body, c)
    return c


def _count(sources, rows, pred):
    c = _row_reduce(sources, rows, lambda blk, col: jnp.where(pred(blk, col), 1.0, 0.0), jnp.add, 0.0)
    return jnp.sum(c, axis=1, keepdims=True)


def _topk_select(sources, rows, ncols_max, topk):
    kf = float(topk)

    def bis(_, carry):
        lo, hi = carry
        mid = (lo >> 1) + (hi >> 1) + (lo & hi & 1)
        thr = _key_to_f32(mid)
        ge = _count(sources, rows, lambda blk, col: blk >= thr) >= kf
        return jnp.where(ge, mid, lo), jnp.where(ge, hi, mid)

    lo0 = jnp.full((rows, 1), KEY_NEG_INF, I32)
    hi0 = jnp.full((rows, 1), KEY_NAN_HI, I32)
    _, hi = lax.fori_loop(0, 32, bis, (lo0, hi0))

    def below(upper):
        m = _row_reduce(sources, rows, lambda blk, col: jnp.where(blk < upper, blk, -jnp.inf),
                        jnp.maximum, -jnp.inf)
        m = jnp.max(m, axis=1, keepdims=True)
        return m, _count(sources, rows, lambda blk, col: blk >= m)

    def short(carry):
        return jnp.max(jnp.where(carry[2] < kf, 1.0, 0.0)) > 0.0

    def lower(carry):
        upper, m, c = carry
        upper = jnp.where(c < kf, m, upper)
        m, c = below(upper)
        return upper, m, c

    upper0 = _key_to_f32(hi)
    m0, c0 = below(upper0)
    _, thr, cnt = lax.while_loop(short, lower, (upper0, m0, c0))
    finite = thr > -jnp.inf
    excess = (cnt > kf) & finite

    def tie_break(_):
        need = kf - _count(sources, rows, lambda blk, col: blk > thr)

        def bis_c(_, carry):
            lo_c, hi_c = carry
            mid = (lo_c + hi_c) >> 1
            c = _count(sources, rows, lambda blk, col: (blk == thr) & (col <= mid))
            ge = c >= need
            return jnp.where(ge, lo_c, mid), jnp.where(ge, mid, hi_c)

        n_it = int(np.ceil(np.log2(ncols_max + 1))) + 1
        _, hi_c = lax.fori_loop(0, n_it, bis_c, (jnp.full((rows, 1), -1, I32),
                                                 jnp.full((rows, 1), ncols_max - 1, I32)))
        return jnp.where(excess, hi_c, INT_MAX)

    any_excess = jnp.max(jnp.where(excess, 1.0, 0.0)) > 0.0
    cidx = lax.cond(any_excess, tie_break, lambda _: jnp.full((rows, 1), INT_MAX, I32), 0)
    return thr, jnp.where(finite, cidx, -1)


def _selected(blk, col, thr, cidx):
    return (blk > thr) | ((blk == thr) & (col <= cidx))


def _attn_prompt_kernel(qpad_ref, qih_ref, qil_ref, wi_ref, kith_ref, kitl_ref, kt_ref, v0_ref, v1_ref, o_ref,
                        sc_ref, thr_ref, cidx_ref, m_ref, acc_ref, *, topk):
    i = pl.program_id(1)
    tq = qpad_ref.shape[1]
    tk = sc_ref.shape[2]
    nkb = (i * tq + tq + tk - 1) // tk
    row_pos = i * tq + lax.broadcasted_iota(I32, (tq, tk), 0)
    wi = wi_ref[0]

    def score_block(j, _):
        ks = pl.ds(pl.multiple_of(j * tk, tk), tk)
        kh = kith_ref[0, :, ks]
        kl = kitl_ref[0, :, ks]
        acc = jnp.zeros((tq, tk), F32)
        for hh in range(IDX_HEADS):
            qh = qih_ref[0, hh]
            ql = qil_ref[0, hh]
            s = _dot(qh, kh) + _dot(ql, kh) + _dot(qh, kl)
            acc = acc + jnp.maximum(s, 0.0) * wi[:, hh:hh + 1]
        col = j * tk + lax.broadcasted_iota(I32, (tq, tk), 1)
        sc_ref[j] = jnp.where(col <= row_pos, acc, -jnp.inf)
        return 0

    lax.fori_loop(0, nkb, score_block, 0)

    for r in range(tq // SEL_ROWS):
        rs = slice(r * SEL_ROWS, (r + 1) * SEL_ROWS)
        src = [(lambda j, rs=rs: sc_ref[j, rs, :], nkb, tk, 0)]
        thr, cidx = _topk_select(src, SEL_ROWS, sc_ref.shape[0] * tk, topk)
        thr_ref[rs, :] = thr
        cidx_ref[rs, :] = cidx

    m_ref[...] = jnp.full(m_ref.shape, NEG, F32)
    acc_ref[...] = jnp.zeros(acc_ref.shape, F32)
    thr = thr_ref[...]
    cidx = cidx_ref[...]

    def attn_block(j, _):
        ks = pl.ds(pl.multiple_of(j * tk, tk), tk)
        col = j * tk + lax.broadcasted_iota(I32, (tq, tk), 1)
        bias = jnp.where(_selected(sc_ref[j], col, thr, cidx), 0.0, NEG)
        kt = kt_ref[0, :, ks]
        for hh in range(ATTN_HEADS):
            vg = (v0_ref, v1_ref)[hh // KV_GROUP][0, ks, :]
            s = _dot(qpad_ref[0, :, hh * LANES:(hh + 1) * LANES], kt) + bias
            m_old = m_ref[hh]
            m_new = jnp.maximum(m_old, jnp.max(s, axis=1, keepdims=True))
            alpha = jnp.exp(m_old - m_new)
            p = jnp.exp(s - jnp.concatenate([m_new] * (tk // LANES), axis=1))
            acc_ref[hh] = alpha * acc_ref[hh] + _dot(p.astype(BF16), vg)
            m_ref[hh] = m_new
        return 0

    lax.fori_loop(0, nkb, attn_block, 0)

    outs = []
    for hh in range(ATTN_HEADS):
        acc = acc_ref[hh]
        outs.append((acc / pltpu.roll(acc, HEAD_DIM, 1))[:, :HEAD_DIM])
    o_ref[0] = jnp.concatenate(outs, axis=1).astype(BF16)


def _attn_prompt(qpad, qih, qil, wi, kith, kitl, ktb, v0, v1, topk):
    b, t, _ = qpad.shape
    tq = Q_TILE
    tk = K_BLOCK
    assert t % tq == 0 and t % tk == 0 and tk % tq == 0
    tile = lambda w: pl.BlockSpec((1, tq, w), lambda bi, i: (bi, i, 0))
    seq = lambda w: pl.BlockSpec((1, t, w), lambda bi, i: (bi, 0, 0))
    seqt = lambda w: pl.BlockSpec((1, w, t), lambda bi, i: (bi, 0, 0))
    hm = pl.BlockSpec((1, IDX_HEADS, tq, IDX_DIM), lambda bi, i: (bi, 0, i, 0))
    return pl.pallas_call(
        functools.partial(_attn_prompt_kernel, topk=topk),
        grid=(b, t // tq),
        in_specs=[tile(ATTN_HEADS * LANES), hm, hm, tile(IDX_HEADS), seqt(IDX_DIM), seqt(IDX_DIM),
                  seqt(LANES), seq(LANES), seq(LANES)],
        out_specs=tile(ATTN_WIDTH),
        out_shape=jax.ShapeDtypeStruct((b, t, ATTN_WIDTH), BF16),
        scratch_shapes=[pltpu.VMEM((t // tk, tq, tk), F32), pltpu.VMEM((tq, 1), F32), pltpu.VMEM((tq, 1), I32),
                        pltpu.VMEM((ATTN_HEADS, tq, LANES), F32), pltpu.VMEM((ATTN_HEADS, tq, LANES), F32)],
        compiler_params=pltpu.CompilerParams(dimension_semantics=("arbitrary", "arbitrary"),
                                             vmem_limit_bytes=VMEM_LIMIT),
        name="attn_prompt",
    )(qpad, qih, qil, wi, kith, kitl, ktb, v0, v1)


def _score_sample_kernel(pt_ref, qih_ref, qil_ref, wcol_ref, knh_ref, knl_ref, *rest):
    pages = rest[:PAGES_PER_STEP]
    scp_ref, scn_ref, kh_s, kl_s = rest[PAGES_PER_STEP:]
    c = pl.program_id(1)
    ds = qih_ref.shape[1]
    page = pages[0].shape[1]
    qh = qih_ref[...].reshape(IDX_HEADS * ds, IDX_DIM)
    ql = qil_ref[...].reshape(IDX_HEADS * ds, IDX_DIM)
    wcol = wcol_ref[0]

    def combine(s):
        r = jnp.maximum(s, 0.0) * wcol
        return jnp.sum(r.reshape(IDX_HEADS, ds, r.shape[1]), axis=0)

    for pg in range(PAGES_PER_STEP):
        kh, kl = _split(pages[pg][...])
        kh_s[:, pg * page:(pg + 1) * page] = kh
        kl_s[:, pg * page:(pg + 1) * page] = kl
    kh = kh_s[...]
    scp_ref[0] = combine(_dot(qh, kh) + _dot(ql, kh) + _dot(qh, kl_s[...]))

    @pl.when(c == pl.num_programs(1) - 1)
    def _():
        knh = knh_ref[0]
        sn = combine(_dot_nt(qh, knh) + _dot_nt(ql, knh) + _dot_nt(qh, knl_ref[0]))
        tok = lax.broadcasted_iota(I32, sn.shape, 0)
        col = lax.broadcasted_iota(I32, sn.shape, 1)
        scn_ref[0] = jnp.where(col <= tok, sn, -jnp.inf)


def _select_sample_kernel(scp_ref, scn_ref, thr_ref, cidx_ref, *, topk, blk_w):
    rows, past = scp_ref.shape
    src = [(lambda j: scp_ref[:, pl.ds(pl.multiple_of(j * blk_w, blk_w), blk_w)], past // blk_w, blk_w, 0),
           (lambda j: scn_ref[...], 1, scn_ref.shape[1], past)]
    thr, cidx = _topk_select(src, rows, past + scn_ref.shape[1], topk)
    thr_ref[...] = thr
    cidx_ref[...] = cidx


def _attn_sample_kernel(pt_ref, qpad_ref, scp_ref, scn_ref, thr_ref, cidx_ref, knew_ref, vnew_ref, *rest,
                        past_len):
    kpages = rest[:PAGES_PER_STEP]
    vpages = rest[PAGES_PER_STEP:2 * PAGES_PER_STEP]
    o_ref, kt_s, vt_s, m_ref, l_ref, acc_ref = rest[2 * PAGES_PER_STEP:]
    c = pl.program_id(1)
    ds = qpad_ref.shape[0]
    page = kpages[0].shape[1]
    thr = thr_ref[...]
    cidx = cidx_ref[...]

    @pl.when(c == 0)
    def _():
        m_ref[...] = jnp.full(m_ref.shape, NEG, F32)
        l_ref[...] = jnp.zeros(l_ref.shape, F32)
        acc_ref[...] = jnp.zeros(acc_ref.shape, F32)

    q_all = jnp.concatenate([qpad_ref[:, hh * LANES:(hh + 1) * LANES] for hh in range(ATTN_HEADS)], axis=0)

    def attend(sc, col, s_of_q, pv):
        bias1 = jnp.where(_selected(sc, col, thr, cidx), 0.0, NEG)
        s = s_of_q + jnp.concatenate([bias1] * ATTN_HEADS, axis=0)
        m_old = m_ref[...]
        m_new = jnp.maximum(m_old, jnp.max(s, axis=1, keepdims=True))
        alpha = jnp.exp(m_old - m_new)
        p = jnp.exp(s - m_new)
        l_ref[...] = alpha * l_ref[...] + jnp.sum(p, axis=1, keepdims=True)
        acc_ref[...] = alpha * acc_ref[...] + pv(p.astype(BF16))
        m_ref[...] = m_new

    for pg in range(PAGES_PER_STEP):
        kt_s[:, pg * page:(pg + 1) * page] = kpages[pg][...].astype(BF16)
        vt_s[:, pg * page:(pg + 1) * page] = vpages[pg][...].astype(BF16)
    width = PAGES_PER_STEP * page
    col = c * width + lax.broadcasted_iota(I32, (ds, width), 1)
    attend(scp_ref[0], col, _dot(q_all, kt_s[...]), lambda p: _dot_nt(p, vt_s[...]))

    @pl.when(c == pl.num_programs(1) - 1)
    def _():
        sc = scn_ref[0]
        ncol = past_len + lax.broadcasted_iota(I32, sc.shape, 1)
        attend(sc, ncol, _dot_nt(q_all, knew_ref[0]), lambda p: _dot(p, vnew_ref[0]))
        o = acc_ref[...] / l_ref[...]
        outs = []
        for hh in range(ATTN_HEADS):
            g = hh // KV_GROUP
            outs.append(o[hh * ds:(hh + 1) * ds, g * HEAD_DIM:(g + 1) * HEAD_DIM])
        o_ref[...] = jnp.concatenate(outs, axis=1).astype(BF16)


def _attn_sample(qpad, qih, qil, wi, kih, kil, kb, vb, pool_kt, pool_vt, pool_ikt, page_table, topk, db):
    n = qpad.shape[0]
    ds = n // db
    n_pool, _, page = pool_ikt.shape
    n_pages = page_table.shape[1]
    past_len = n_pages * page
    pps = PAGES_PER_STEP
    n_chunks = n_pages // pps
    assert n_pages % pps == 0 and ds <= LANES

    wcol = wi.reshape(db, ds, IDX_HEADS).transpose(0, 2, 1).reshape(db, IDX_HEADS * ds, 1)
    padk = lambda a: jnp.pad(a.reshape(db, ds, a.shape[1]), ((0, 0), (0, LANES - ds), (0, 0)))
    knh, knl, knew, vnew = padk(kih), padk(kil), padk(kb), padk(vb)

    def page_spec(w, pg):
        return pl.BlockSpec((None, w, page), lambda b, c, pt: (pt[b, c * pps + pg], 0, 0))

    hm = pl.BlockSpec((IDX_HEADS, ds, IDX_DIM), lambda b, c, pt: (0, b, 0))
    seq3 = lambda r, w: pl.BlockSpec((1, r, w), lambda b, c, pt: (b, 0, 0))
    chunk = pl.BlockSpec((1, ds, pps * page), lambda b, c, pt: (b, 0, c))
    scp, scn = pl.pallas_call(
        _score_sample_kernel,
        grid_spec=pltpu.PrefetchScalarGridSpec(
            num_scalar_prefetch=1, grid=(db, n_chunks),
            in_specs=[hm, hm, seq3(IDX_HEADS * ds, 1), seq3(LANES, IDX_DIM), seq3(LANES, IDX_DIM)]
                     + [page_spec(IDX_DIM, pg) for pg in range(pps)],
            out_specs=[chunk, seq3(ds, LANES)],
            scratch_shapes=[pltpu.VMEM((IDX_DIM, pps * page), BF16), pltpu.VMEM((IDX_DIM, pps * page), BF16)]),
        out_shape=[jax.ShapeDtypeStruct((db, ds, past_len), F32), jax.ShapeDtypeStruct((db, ds, LANES), F32)],
        compiler_params=pltpu.CompilerParams(dimension_semantics=("arbitrary", "arbitrary"),
                                             vmem_limit_bytes=VMEM_LIMIT),
        name="score_sample",
    )(page_table, qih, qil, wcol, knh, knl, *([pool_ikt] * pps))

    rows = min(SEL_ROWS, n)
    blk_w = 512
    thr, cidx = pl.pallas_call(
        functools.partial(_select_sample_kernel, topk=topk, blk_w=blk_w),
        grid=(n // rows,),
        in_specs=[pl.BlockSpec((rows, past_len), lambda r: (r, 0)), pl.BlockSpec((rows, LANES), lambda r: (r, 0))],
        out_specs=[pl.BlockSpec((rows, 1), lambda r: (r, 0)), pl.BlockSpec((rows, 1), lambda r: (r, 0))],
        out_shape=[jax.ShapeDtypeStruct((n, 1), F32), jax.ShapeDtypeStruct((n, 1), I32)],
        compiler_params=pltpu.CompilerParams(dimension_semantics=("arbitrary",), vmem_limit_bytes=VMEM_LIMIT),
        name="select_sample",
    )(scp.reshape(n, past_len), scn.reshape(n, LANES))

    tokrows = lambda w: pl.BlockSpec((ds, w), lambda b, c, pt: (b, 0))
    nrow = ATTN_HEADS * ds
    return pl.pallas_call(
        functools.partial(_attn_sample_kernel, past_len=past_len),
        grid_spec=pltpu.PrefetchScalarGridSpec(
            num_scalar_prefetch=1, grid=(db, n_chunks),
            in_specs=[tokrows(ATTN_HEADS * LANES), chunk, seq3(ds, LANES),
                      tokrows(1), tokrows(1), seq3(LANES, LANES), seq3(LANES, LANES)]
                     + [page_spec(LANES, pg) for pg in range(pps)] * 2,
            out_specs=tokrows(ATTN_WIDTH),
            scratch_shapes=[pltpu.VMEM((LANES, pps * page), BF16), pltpu.VMEM((LANES, pps * page), BF16),
                            pltpu.VMEM((nrow, 1), F32), pltpu.VMEM((nrow, 1), F32),
                            pltpu.VMEM((nrow, LANES), F32)]),
        out_shape=jax.ShapeDtypeStruct((n, ATTN_WIDTH), BF16),
        compiler_params=pltpu.CompilerParams(dimension_semantics=("arbitrary", "arbitrary"),
                                             vmem_limit_bytes=VMEM_LIMIT),
        name="attn_sample",
    )(page_table, qpad, scp, scn, thr, cidx, knew, vnew, *([pool_kt] * pps), *([pool_vt] * pps))


def _mix_mlp_kernel(a_ref, c_ref, x_ref, woa_ref, woc_ref, g1_ref, g2_ref, wup_ref, wdn_ref, g3_ref, y_ref):
    m = _dot(a_ref[...], woa_ref[...]) + _dot(c_ref[...], woc_ref[...])
    x1 = x_ref[...] + _rms(m) * g1_ref[...]
    h = (_rms(x1) * g2_ref[...]).astype(BF16)
    dff = wup_ref.shape[1]
    f = jnp.zeros(x1.shape, F32)
    for cc in range(dff // FF_CHUNK):
        cs = slice(cc * FF_CHUNK, (cc + 1) * FF_CHUNK)
        up = jnp.maximum(_dot(h, wup_ref[:, cs]), 0.0)
        f = f + _dot((up * up).astype(BF16), wdn_ref[cs, :])
    y_ref[...] = x1 + _rms(f) * g3_ref[...]


def _mix_mlp(a, c, x, woa, woc, g1, g2, wup, wdn, g3):
    n, d = x.shape
    tm = min(MLP_TILE, n)
    full = lambda w: pl.BlockSpec(w.shape, lambda i: (0,) * w.ndim)
    tok = lambda w: pl.BlockSpec((tm, w), lambda i: (i, 0))
    return pl.pallas_call(
        _mix_mlp_kernel,
        grid=(n // tm,),
        in_specs=[tok(a.shape[1]), tok(c.shape[1]), tok(d), full(woa), full(woc), full(g1), full(g2),
                  full(wup), full(wdn), full(g3)],
        out_specs=tok(d),
        out_shape=jax.ShapeDtypeStruct((n, d), F32),
        compiler_params=pltpu.CompilerParams(dimension_semantics=("arbitrary",), vmem_limit_bytes=VMEM_LIMIT),
        name="mix_mlp",
    )(a, c, x, woa, woc, g1, g2, wup, wdn, g3)


def _rope_tables(pos):
    rot = HEAD_DIM // 4
    half = rot // 2
    inv_freq = jnp.power(ROPE_THETA, -jnp.arange(half, dtype=F32) * (2.0 / rot))
    ang = pos.astype(F32)[:, None] * inv_freq[None, :]
    cos, sin = jnp.cos(ang), jnp.sin(ang)
    n = pos.shape[0]
    zeros = lambda w: jnp.zeros((n, w), F32)
    reps = LANES // HEAD_DIM
    rc = jnp.concatenate([cos, cos, jnp.ones((n, HEAD_DIM - rot), F32)] * reps, axis=1)
    ra = jnp.concatenate([zeros(half), sin, zeros(HEAD_DIM - rot)] * reps, axis=1)
    rb = jnp.concatenate([-sin, zeros(HEAD_DIM - half)] * reps, axis=1)
    return rc, ra, rb


def _prep_w_in(w):
    d = w.shape[0]
    sizes = (ATTN_WIDTH, N_KV_HEADS * HEAD_DIM, N_KV_HEADS * HEAD_DIM, IDX_HEADS * IDX_DIM, IDX_DIM, IDX_HEADS)
    offs = np.cumsum((0,) + sizes)
    wq, wk, wv, wqi, wki, wwi = [w[:, offs[i]:offs[i + 1]] for i in range(6)]
    wu = w[:, offs[6]:]
    zpad = jnp.zeros((d, HEAD_DIM), w.dtype)
    qcols = []
    for hh in range(ATTN_HEADS):
        piece = wq[:, hh * HEAD_DIM:(hh + 1) * HEAD_DIM]
        qcols += [piece, zpad] if hh // KV_GROUP == 0 else [zpad, piece]
    wm = jnp.concatenate(qcols + [wk, wv, wu], axis=1).astype(BF16)
    widx = jnp.concatenate([wqi, wki, wwi, jnp.zeros((d, LANES - IDX_DIM - IDX_HEADS), w.dtype)], axis=1)
    wih, wil = _split(widx)
    return wm, wih, wil


def kernel(x_prompt, x_sample, cache_k, cache_v, cache_idx_k, state_conv, page_table, norm_mix_pre, w_in,
           conv_dw_w, conv_dw_b, conv_ln_g, conv_ln_b, w_out, norm_mix_post, norm_mlp_pre, w_up, w_down,
           norm_mlp_post):
    depth = w_in.shape[0]
    b, t, d = x_prompt.shape
    db, ds, _ = x_sample.shape
    n_pool, page = cache_k.shape[1], cache_k.shape[2]
    past_len = page_table.shape[1] * page
    topk_p = min(INDEX_TOPK, t // 4)
    topk_s = min(INDEX_TOPK, (past_len + ds) // 4)
    hist = CONV_WIDTH - 1

    rcp, rap, rbp = _rope_tables(jnp.arange(t, dtype=I32))
    rcs, ras, rbs = [jnp.tile(a, (db, 1)) for a in _rope_tables(past_len + jnp.arange(ds, dtype=I32))]

    xp = x_prompt
    xs = x_sample.reshape(db * ds, d)
    outs = [[] for _ in range(8)]
    row = lambda a: a.reshape(1, -1)
    for l in range(depth):
        wm, wih, wil = _prep_w_in(w_in[l])
        conv_args = (conv_dw_w[l], row(conv_dw_b[l]), row(conv_ln_g[l]), row(conv_ln_b[l]))
        woa = w_out[l, :ATTN_WIDTH].astype(BF16)
        woc = w_out[l, ATTN_WIDTH:].astype(BF16)
        mlp_args = (woa, woc, row(norm_mix_post[l]), row(norm_mlp_pre[l]), w_up[l].astype(BF16),
                    w_down[l].astype(BF16), row(norm_mlp_post[l]))

        (qpad, kt32, vt32, ktb, v0, v1, kit32, kith, kitl, qih, qil, wi, co, ctail) = _inproj_prompt(
            xp, row(norm_mix_pre[l]), wm, wih, wil, rcp, rap, rbp, *conv_args)
        ao = _attn_prompt(qpad, qih, qil, wi, kith, kitl, ktb, v0, v1, topk_p)
        xp = _mix_mlp(ao.reshape(b * t, -1), co.reshape(b * t, -1), xp.reshape(b * t, d), *mlp_args).reshape(b, t, d)
        unt = lambda a: a.reshape(b, N_KV_HEADS, HEAD_DIM, t).transpose(0, 3, 1, 2)
        outs[0].append(unt(kt32))
        outs[1].append(unt(vt32))
        outs[2].append(kit32.transpose(0, 2, 1))
        outs[3].append(ctail[:, HALO - hist:])

        pool_kt = cache_k[l].transpose(0, 2, 3, 1).reshape(n_pool, N_KV_HEADS * HEAD_DIM, page)
        pool_vt = cache_v[l].transpose(0, 2, 3, 1).reshape(n_pool, N_KV_HEADS * HEAD_DIM, page)
        pool_ikt = cache_idx_k[l].transpose(0, 2, 1)
        (qpad, k32, v32, kb, vb, ki32, kih, kil, qih, qil, wi, co, cnew) = _inproj_sample(
            xs, row(norm_mix_pre[l]), wm, wih, wil, rcs, ras, rbs, *conv_args, state_conv[l].transpose(1, 0, 2))
        ao = _attn_sample(qpad, qih, qil, wi, kih, kil, kb, vb, pool_kt, pool_vt, pool_ikt,
                          page_table, topk_s, db)
        xs = _mix_mlp(ao, co, xs, *mlp_args)
        outs[4].append(k32.reshape(db, ds, N_KV_HEADS, HEAD_DIM))
        outs[5].append(v32.reshape(db, ds, N_KV_HEADS, HEAD_DIM))
        outs[6].append(ki32.reshape(db, ds, IDX_DIM))
        outs[7].append(cnew.transpose(1, 0, 2))

    st = [jnp.stack(o, 0) for o in outs]
    return (xp, xs.reshape(db, ds, d), st[0], st[1], st[2], st[3], st[4], st[5], st[6], st[7])
```

```python
import functools

import numpy as np
import jax
import jax.numpy as jnp
from jax import lax
from jax.experimental import pallas as pl
from jax.experimental.pallas import tpu as pltpu

F32 = jnp.float32
BF16 = jnp.bfloat16
I32 = jnp.int32

ATTN_HEADS = 8
HEAD_DIM = 64
ATTN_WIDTH = ATTN_HEADS * HEAD_DIM
N_KV_HEADS = 2
KV_GROUP = ATTN_HEADS // N_KV_HEADS
IDX_HEADS = 8
IDX_DIM = 64
INDEX_TOPK = 256
ROPE_THETA = 500000.0
CONV_WIDTH = 31
RMS_EPS = 1e-6
LN_EPS = 1e-5
SCALE = 0.125
assert HEAD_DIM ** -0.5 == SCALE and IDX_DIM ** -0.5 == SCALE
assert N_KV_HEADS * HEAD_DIM == 128

LANES = 128
LOG2E = 1.4426950408889634
NEG = -(2.0 ** 100)
HALO = 32
TOK_TILE = 256
Q_TILE = 256
K_BLOCK = 512
SEL_ROWS = 128
BISECT_STEPS = 28
MLP_TILE = 256
FF_CHUNK = 1024
PAGES_PER_STEP = 16
VMEM_LIMIT = 56 * 1024 * 1024

KEY_NEG_INF = -2139095041
KEY_NAN_HI = 2139095041
INT_MAX = 2147483647


def _dot(a, b):
    return jnp.dot(a, b, preferred_element_type=F32)


def _dot_nt(a, b):
    return lax.dot_general(a, b, (((1,), (1,)), ((), ())), preferred_element_type=F32)


def _split(x):
    hi = x.astype(BF16)
    lo = (x - hi.astype(F32)).astype(BF16)
    return hi, lo


def _rms(x):
    return x * lax.rsqrt(jnp.mean(x * x, axis=-1, keepdims=True) + RMS_EPS)


def _sigmoid(x):
    return 1.0 / (1.0 + jnp.exp(-x))


def _inproj(x, g_pre, wm_ref, wih_ref, wil_ref, rc, ra, rb):
    h = _rms(x) * g_pre
    h_hi, h_lo = _split(h)
    zm = _dot(h_hi, wm_ref[...])
    wih = wih_ref[...]
    zi = _dot(h_hi, wih) + _dot(h_lo, wih) + _dot(h_hi, wil_ref[...])

    def rope(c):
        return c * rc + pltpu.roll(c, 8, 1) * ra + pltpu.roll(c, LANES - 8, 1) * rb

    nq = ATTN_HEADS * LANES
    q = [rope(zm[:, c * LANES:(c + 1) * LANES]) * (SCALE * LOG2E) for c in range(ATTN_HEADS)]
    k = rope(zm[:, nq:nq + LANES])
    v = zm[:, nq + LANES:nq + 2 * LANES]
    cch = zm.shape[1] - nq - 2 * LANES
    ua = zm[:, nq + 2 * LANES:nq + 2 * LANES + cch // 2]
    ub = zm[:, nq + 2 * LANES + cch // 2:]
    u = ua * _sigmoid(ub)
    qi = []
    for c in range(IDX_HEADS * IDX_DIM // LANES):
        r = rope(zi[:, c * LANES:(c + 1) * LANES]) * SCALE
        qi.append(r[:, :IDX_DIM])
        qi.append(r[:, IDX_DIM:])
    kc = zi[:, IDX_HEADS * IDX_DIM:IDX_HEADS * IDX_DIM + LANES]
    kic = rope(kc)
    wi = kc[:, IDX_DIM:IDX_DIM + IDX_HEADS]
    return dict(q=q, k=k, v=v, u=u, qi=qi, kic=kic, wi=wi)


def _conv_post(acc, lng, lnb):
    mu = jnp.mean(acc, axis=-1, keepdims=True)
    d = acc - mu
    var = jnp.mean(d * d, axis=-1, keepdims=True)
    y = d * lax.rsqrt(var + LN_EPS) * lng + lnb
    return y * _sigmoid(y)


def _inproj_prompt_kernel(x_ref, g_ref, wm_ref, wih_ref, wil_ref, rc_ref, ra_ref, rb_ref,
                          cw_ref, cb_ref, lng_ref, lnb_ref,
                          qpad_ref, kt32_ref, vt32_ref, ktb_ref, v0_ref, v1_ref, kit32_ref, kitb_ref,
                          qib_ref, wi_ref, co_ref, ctail_ref, ubuf):
    i = pl.program_id(1)
    tt = x_ref.shape[1]
    p = _inproj(x_ref[0], g_ref[...], wm_ref, wih_ref, wil_ref, rc_ref[...], ra_ref[...], rb_ref[...])
    for hh in range(ATTN_HEADS):
        qpad_ref[0, :, hh * LANES:(hh + 1) * LANES] = p["q"][hh].astype(BF16)
    kt = p["k"].T
    kt32_ref[0] = kt
    ktb_ref[0] = kt.astype(BF16)
    v = p["v"]
    vt32_ref[0] = v.T
    lane = lax.broadcasted_iota(I32, v.shape, 1)
    v0_ref[0] = jnp.where(lane < HEAD_DIM, v, 1.0).astype(BF16)
    v1_ref[0] = jnp.where(lane < HEAD_DIM, pltpu.roll(v, HEAD_DIM, 1), 1.0).astype(BF16)
    kit = p["kic"].T[:IDX_DIM]
    kit32_ref[0] = kit
    kitb_ref[0] = kit.astype(BF16)
    wi_ref[0] = p["wi"]
    for hh in range(IDX_HEADS):
        qib_ref[0, hh] = p["qi"][hh].astype(BF16)

    @pl.when(i == 0)
    def _():
        ubuf[0:HALO, :] = jnp.zeros((HALO, ubuf.shape[1]), F32)

    @pl.when(i > 0)
    def _():
        ubuf[0:HALO, :] = ubuf[tt:tt + HALO, :]

    ubuf[HALO:HALO + tt, :] = p["u"]
    ctail_ref[0] = ubuf[tt:tt + HALO, :]

    rc_rows = 64
    cb = cb_ref[...]
    lng = lng_ref[...]
    lnb = lnb_ref[...]
    off = HALO - (CONV_WIDTH - 1)
    for r in range(tt // rc_rows):
        acc = jnp.broadcast_to(cb, (rc_rows, cb.shape[1]))
        for j in range(CONV_WIDTH):
            acc = acc + ubuf[pl.ds(off + j + r * rc_rows, rc_rows), :] * cw_ref[j:j + 1, :]
        co_ref[0, r * rc_rows:(r + 1) * rc_rows, :] = _conv_post(acc, lng, lnb).astype(BF16)


def _inproj_sample_kernel(x_ref, g_ref, wm_ref, wih_ref, wil_ref, rc_ref, ra_ref, rb_ref,
                          cw_ref, cb_ref, lng_ref, lnb_ref, st_ref,
                          qpad_ref, k32_ref, v32_ref, kb_ref, vb_ref, ki32_ref, kih_ref, kil_ref,
                          qih_ref, qil_ref, wi_ref, co_ref, cnew_ref, fbuf, uscr):
    hist, db, cch = st_ref.shape
    ds = x_ref.shape[0] // db
    p = _inproj(x_ref[...], g_ref[...], wm_ref, wih_ref, wil_ref, rc_ref[...], ra_ref[...], rb_ref[...])
    for hh in range(ATTN_HEADS):
        qpad_ref[:, hh * LANES:(hh + 1) * LANES] = p["q"][hh].astype(BF16)
    k32_ref[...] = p["k"]
    v32_ref[...] = p["v"]
    kb_ref[...] = p["k"].astype(BF16)
    vb_ref[...] = p["v"].astype(BF16)
    ki = p["kic"][:, :IDX_DIM]
    ki32_ref[...] = ki
    kih, kil = _split(ki)
    kih_ref[...] = kih
    kil_ref[...] = kil
    wi_ref[...] = p["wi"]
    for hh in range(IDX_HEADS):
        hi, lo = _split(p["qi"][hh])
        qih_ref[hh] = hi
        qil_ref[hh] = lo

    nch = cch // LANES
    u = p["u"]
    for cc in range(nch):
        uscr[cc] = u[:, cc * LANES:(cc + 1) * LANES]
    fbuf[0:hist] = st_ref[...]
    for t in range(ds):
        for cc in range(nch):
            fbuf[hist + t, :, cc * LANES:(cc + 1) * LANES] = uscr[cc, pl.ds(t, db, stride=ds), :]
    acc = jnp.broadcast_to(cb_ref[...].reshape(1, 1, cch), (ds, db, cch))
    for j in range(CONV_WIDTH):
        acc = acc + fbuf[j:j + ds] * cw_ref[j:j + 1, :].reshape(1, 1, cch)
    y = _conv_post(acc.reshape(ds * db, cch), lng_ref[...], lnb_ref[...])
    for t in range(ds):
        for cc in range(nch):
            uscr[cc, pl.ds(t, db, stride=ds), :] = y[t * db:(t + 1) * db, cc * LANES:(cc + 1) * LANES]
    for cc in range(nch):
        co_ref[:, cc * LANES:(cc + 1) * LANES] = uscr[cc].astype(BF16)
    cnew_ref[...] = fbuf[ds:ds + hist]


def _inproj_prompt(x, g_pre, wm, wih, wil, rc, ra, rb, cw, cb, lng, lnb):
    b, t, d = x.shape
    tt = TOK_TILE
    cch = cw.shape[1]
    full = lambda a: pl.BlockSpec(a.shape, lambda bi, i: (0,) * a.ndim)
    tok = lambda w: pl.BlockSpec((1, tt, w), lambda bi, i: (bi, i, 0))
    feat = lambda w: pl.BlockSpec((1, w, tt), lambda bi, i: (bi, 0, i))
    tab = pl.BlockSpec((tt, LANES), lambda bi, i: (i, 0))
    hm = pl.BlockSpec((1, IDX_HEADS, tt, IDX_DIM), lambda bi, i: (bi, 0, i, 0))
    sds = jax.ShapeDtypeStruct
    out_shape = [sds((b, t, ATTN_HEADS * LANES), BF16), sds((b, LANES, t), F32), sds((b, LANES, t), F32),
                 sds((b, LANES, t), BF16), sds((b, t, LANES), BF16), sds((b, t, LANES), BF16),
                 sds((b, IDX_DIM, t), F32), sds((b, IDX_DIM, t), BF16),
                 sds((b, IDX_HEADS, t, IDX_DIM), BF16),
                 sds((b, t, IDX_HEADS), F32), sds((b, t, cch), BF16), sds((b, HALO, cch), F32)]
    out_specs = [tok(ATTN_HEADS * LANES), feat(LANES), feat(LANES), feat(LANES), tok(LANES), tok(LANES),
                 feat(IDX_DIM), feat(IDX_DIM), hm, tok(IDX_HEADS), tok(cch),
                 pl.BlockSpec((1, HALO, cch), lambda bi, i: (bi, 0, 0))]
    return pl.pallas_call(
        _inproj_prompt_kernel,
        grid=(b, t // tt),
        in_specs=[tok(d), full(g_pre), full(wm), full(wih), full(wil), tab, tab, tab,
                  full(cw), full(cb), full(lng), full(lnb)],
        out_specs=out_specs,
        out_shape=out_shape,
        scratch_shapes=[pltpu.VMEM((tt + HALO, cch), F32)],
        compiler_params=pltpu.CompilerParams(dimension_semantics=("arbitrary", "arbitrary"),
                                             vmem_limit_bytes=VMEM_LIMIT),
        name="inproj_prompt",
    )(x, g_pre, wm, wih, wil, rc, ra, rb, cw, cb, lng, lnb)


def _inproj_sample(x, g_pre, wm, wih, wil, rc, ra, rb, cw, cb, lng, lnb, state):
    n, d = x.shape
    hist, db, cch = state.shape
    ds = n // db
    sds = jax.ShapeDtypeStruct
    out_shape = [sds((n, ATTN_HEADS * LANES), BF16), sds((n, LANES), F32), sds((n, LANES), F32),
                 sds((n, LANES), BF16), sds((n, LANES), BF16),
                 sds((n, IDX_DIM), F32), sds((n, IDX_DIM), BF16), sds((n, IDX_DIM), BF16),
                 sds((IDX_HEADS, n, IDX_DIM), BF16), sds((IDX_HEADS, n, IDX_DIM), BF16),
                 sds((n, IDX_HEADS), F32), sds((n, cch), BF16), sds((hist, db, cch), F32)]
    return pl.pallas_call(
        _inproj_sample_kernel,
        out_shape=out_shape,
        scratch_shapes=[pltpu.VMEM((hist + ds, db, cch), F32), pltpu.VMEM((cch // LANES, n, LANES), F32)],
        compiler_params=pltpu.CompilerParams(vmem_limit_bytes=VMEM_LIMIT),
        name="inproj_sample",
    )(x, g_pre, wm, wih, wil, rc, ra, rb, cw, cb, lng, lnb, state)


def _key_to_f32(key):
    bits = jnp.where(key < 0, key ^ INT_MAX, key)
    return lax.bitcast_convert_type(bits, F32)


def _row_reduce(sources, rows, fn, op, init):
    c = jnp.full((rows, LANES), init, F32)
    for get, nblk, width, col0 in sources:
        def body(j, c, get=get, width=width, col0=col0):
            blk = get(j)
            col = col0 + j * width + lax.broadcasted_iota(I32, blk.shape, 1)
            val = fn(blk, col)
            for cc in range(width // LANES):
                c = op(c, val[:, cc * LANES:(cc + 1) * LANES])
            return c
        c = lax.fori_loop(0, nblk, body, c)
    return c


def _count(sources, rows, pred):
    c = _row_reduce(sources, rows, lambda blk, col: jnp.where(pred(blk, col), 1.0, 0.0), jnp.add, 0.0)
    return jnp.sum(c, axis=1, keepdims=True)


def _topk_select(sources, rows, topk, vmin, vmax, nvis, cpos, cnn):
    kf = float(topk)
    few = nvis <= kf
    zero_top = (cpos <= kf) & (cnn >= kf) & jnp.logical_not(few)
    up = cpos > kf

    def unsettled(carry):
        return (carry[0] < BISECT_STEPS) & (jnp.min(carry[4]) < 0.5)

    def step(carry):
        it, lo, hi, thr, done = carry
        mid = 0.5 * lo + 0.5 * hi
        c = _count(sources, rows, lambda blk, col: blk > mid)
        hit = (c == kf) & (done < 0.5)
        return (it + 1, jnp.where(c > kf, mid, lo), jnp.where(c < kf, mid, hi),
                jnp.where(hit, mid, thr), jnp.where(hit, 1.0, done))

    carry0 = (jnp.int32(0), jnp.where(up, 0.0, vmin), jnp.where(up, vmax, 0.0),
              jnp.where(few, -jnp.inf, 0.0), jnp.where(few | zero_top, 1.0, 0.0))
    _, _, _, thr, done = lax.while_loop(unsettled, step, carry0)
    settled = done > 0.5
    zero_need = kf - cpos
    cidx = jnp.where(zero_top & (zero_need > 0.0), INT_MAX, -1)
    need = jnp.where(zero_top, zero_need, 0.0)
    cut = jnp.where(zero_top & (cnn > kf) & (zero_need > 0.0), 1.0, 0.0)

    def exact(_):
        thr_e, cnt_e = _kth_largest_exact(sources, rows, topk)
        need_e = kf - _count(sources, rows, lambda blk, col: blk > thr_e)
        fin = thr_e > -jnp.inf
        return (jnp.where(settled, thr, thr_e), jnp.where(settled, cidx, jnp.where(fin, INT_MAX, -1)),
                jnp.where(settled, need, need_e), jnp.where(settled, cut, jnp.where(fin & (cnt_e > kf), 1.0, 0.0)))

    return lax.cond(jnp.min(done) < 0.5, exact, lambda _: (thr, cidx, need, cut), 0)


def _tie_cutoffs(slab_sources, nrows, ncols_max, thr_ref, need_ref, cut_ref, cidx_ref):
    n_it = int(np.ceil(np.log2(ncols_max + 1))) + 1

    def slab(sl, _):
        rs = pl.ds(pl.multiple_of(sl * 8, 8), 8)
        cut = cut_ref[rs, :] > 0.5

        @pl.when(jnp.max(cut_ref[rs, :]) > 0.5)
        def _():
            thr = thr_ref[rs, :]
            need = need_ref[rs, :]
            src = slab_sources(pl.multiple_of(sl * 8, 8))

            def bis_c(_, carry):
                lo_c, hi_c = carry
                mid = (lo_c + hi_c) >> 1
                ge = _count(src, 8, lambda blk, col: (blk == thr) & (col <= mid)) >= need
                return jnp.where(ge, lo_c, mid), jnp.where(ge, mid, hi_c)

            _, hi_c = lax.fori_loop(0, n_it, bis_c, (jnp.full((8, 1), -1, I32),
                                                     jnp.full((8, 1), ncols_max - 1, I32)))
            cidx_ref[rs, :] = jnp.where(cut, hi_c, cidx_ref[rs, :])
        return 0

    lax.fori_loop(0, nrows // 8, slab, 0)


def _drop_excess_ties(get_blk, put_blk, nblk, width, rows, thr, need, cut):
    tri = jnp.where(lax.broadcasted_iota(I32, (width, width), 0) <= lax.broadcasted_iota(I32, (width, width), 1),
                    1.0, 0.0).astype(BF16)
    marked = cut > 0.5

    def body(j, before):
        blk = get_blk(j)
        tied = (blk == thr) & marked
        rank = _dot(jnp.where(tied, 1.0, 0.0).astype(BF16), tri)
        put_blk(j, jnp.where(tied & (before + rank > need), -jnp.inf, blk))
        return before + jnp.max(rank, axis=1, keepdims=True)

    lax.fori_loop(0, nblk, body, jnp.zeros((rows, 1), F32))


def _kth_largest_exact(sources, rows, topk):
    kf = float(topk)

    def bis(_, carry):
        lo, hi = carry
        mid = (lo >> 1) + (hi >> 1) + (lo & hi & 1)
        thr = _key_to_f32(mid)
        ge = _count(sources, rows, lambda blk, col: blk >= thr) >= kf
        return jnp.where(ge, mid, lo), jnp.where(ge, hi, mid)

    lo0 = jnp.full((rows, 1), KEY_NEG_INF, I32)
    hi0 = jnp.full((rows, 1), KEY_NAN_HI, I32)
    _, hi = lax.fori_loop(0, 32, bis, (lo0, hi0))

    def below(upper):
        m = _row_reduce(sources, rows, lambda blk, col: jnp.where(blk < upper, blk, -jnp.inf),
                        jnp.maximum, -jnp.inf)
        m = jnp.max(m, axis=1, keepdims=True)
        return m, _count(sources, rows, lambda blk, col: blk >= m)

    def short(carry):
        return jnp.max(jnp.where(carry[2] < kf, 1.0, 0.0)) > 0.0

    def lower(carry):
        upper, m, c = carry
        upper = jnp.where(c < kf, m, upper)
        m, c = below(upper)
        return upper, m, c

    upper0 = _key_to_f32(hi)
    m0, c0 = below(upper0)
    _, thr, cnt = lax.while_loop(short, lower, (upper0, m0, c0))
    return thr, cnt


def _selected(blk, col, thr, cidx):
    return (blk > thr) | ((blk == thr) & (col <= cidx))


def _attn_prompt_kernel(qpad_ref, qib_ref, wi_ref, kitb_ref, kt_ref, v0_ref, v1_ref, o_ref,
                        sc_ref, vmx_ref, vmn_ref, cp_ref, cn_ref, thr_ref, cidx_ref, m_ref, acc_ref, *, topk):
    i = pl.program_id(1)
    tq = qpad_ref.shape[1]
    tk = sc_ref.shape[2]
    nch = tk // LANES
    nkb = (i * tq + tq + tk - 1) // tk
    row_pos = i * tq + lax.broadcasted_iota(I32, (tq, tk), 0)
    wi = wi_ref[0]

    vmx_ref[...] = jnp.full(vmx_ref.shape, -jnp.inf, F32)
    vmn_ref[...] = jnp.full(vmn_ref.shape, jnp.inf, F32)
    cp_ref[...] = jnp.zeros(cp_ref.shape, F32)
    cn_ref[...] = jnp.zeros(cn_ref.shape, F32)

    def score_block(j, _):
        kb = kitb_ref[0, :, pl.ds(pl.multiple_of(j * tk, tk), tk)]
        acc = jnp.zeros((tq, tk), F32)
        for hh in range(IDX_HEADS):
            acc = acc + jnp.maximum(_dot(qib_ref[0, hh], kb), 0.0) * wi[:, hh:hh + 1]
        vis = j * tk + lax.broadcasted_iota(I32, (tq, tk), 1) <= row_pos
        sc = jnp.where(vis, acc, -jnp.inf)
        sc_ref[j] = sc
        lowest = jnp.where(vis, acc, jnp.inf)
        pos = jnp.where(sc > 0.0, 1.0, 0.0)
        nonneg = jnp.where(sc >= 0.0, 1.0, 0.0)
        mx, mn, cp, cn = vmx_ref[...], vmn_ref[...], cp_ref[...], cn_ref[...]
        for cc in range(nch):
            cs = slice(cc * LANES, (cc + 1) * LANES)
            mx = jnp.maximum(mx, sc[:, cs])
            mn = jnp.minimum(mn, lowest[:, cs])
            cp = cp + pos[:, cs]
            cn = cn + nonneg[:, cs]
        vmx_ref[...] = mx
        vmn_ref[...] = mn
        cp_ref[...] = cp
        cn_ref[...] = cn
        return 0

    lax.fori_loop(0, nkb, score_block, 0)

    for r in range(tq // SEL_ROWS):
        rs = slice(r * SEL_ROWS, (r + 1) * SEL_ROWS)
        src = [(lambda j, rs=rs: sc_ref[j, rs, :], nkb, tk, 0)]
        nvis = (i * tq + r * SEL_ROWS + 1 + lax.broadcasted_iota(I32, (SEL_ROWS, 1), 0)).astype(F32)
        thr, cidx, need, cut = _topk_select(
            src, SEL_ROWS, topk, jnp.min(vmn_ref[rs, :], axis=1, keepdims=True),
            jnp.max(vmx_ref[rs, :], axis=1, keepdims=True), nvis,
            jnp.sum(cp_ref[rs, :], axis=1, keepdims=True), jnp.sum(cn_ref[rs, :], axis=1, keepdims=True))
        thr_ref[rs, :] = thr
        cidx_ref[rs, :] = cidx

        @pl.when(jnp.max(cut) > 0.5)
        def _(rs=rs, thr=thr, need=need, cut=cut):
            def put(j, val):
                sc_ref[j, rs, :] = val
            _drop_excess_ties(lambda j: sc_ref[j, rs, :], put, nkb, tk, SEL_ROWS, thr, need, cut)

    m_ref[...] = jnp.full(m_ref.shape, NEG, F32)
    acc_ref[...] = jnp.zeros(acc_ref.shape, F32)
    thr = thr_ref[...]
    cidx = cidx_ref[...]

    def attn_block(j, _):
        ks = pl.ds(pl.multiple_of(j * tk, tk), tk)
        col = j * tk + lax.broadcasted_iota(I32, (tq, tk), 1)
        bias = jnp.where(_selected(sc_ref[j], col, thr, cidx), 0.0, NEG).astype(BF16)
        kt = kt_ref[0, :, ks]
        qk = lambda hh: _dot(qpad_ref[0, :, hh * LANES:(hh + 1) * LANES], kt)
        s_next = qk(0)
        for hh in range(ATTN_HEADS):
            vg = (v0_ref, v1_ref)[hh // KV_GROUP][0, ks, :]
            s = s_next.astype(BF16) + bias
            if hh + 1 < ATTN_HEADS:
                s_next = qk(hh + 1)
            blk_max = s[:, :LANES]
            for cc in range(1, nch):
                blk_max = jnp.maximum(blk_max, s[:, cc * LANES:(cc + 1) * LANES])
            m_old = m_ref[hh]
            m_new = jnp.maximum(m_old, jnp.max(blk_max.astype(F32), axis=1, keepdims=True))
            alpha = jnp.exp2(m_old - m_new)
            p = jnp.exp2(s - jnp.concatenate([m_new.astype(BF16)] * nch, axis=1))
            acc_ref[hh] = alpha * acc_ref[hh] + _dot(p, vg)
            m_ref[hh] = m_new
        return 0

    lax.fori_loop(0, nkb, attn_block, 0)

    outs = []
    for hh in range(ATTN_HEADS):
        acc = acc_ref[hh]
        outs.append((acc / pltpu.roll(acc, HEAD_DIM, 1))[:, :HEAD_DIM])
    o_ref[0] = jnp.concatenate(outs, axis=1).astype(BF16)


def _attn_prompt(qpad, qib, wi, kitb, ktb, v0, v1, topk):
    b, t, _ = qpad.shape
    tq = Q_TILE
    tk = K_BLOCK
    assert t % tq == 0 and t % tk == 0 and tk % tq == 0
    tile = lambda w: pl.BlockSpec((1, tq, w), lambda bi, i: (bi, i, 0))
    seq = lambda w: pl.BlockSpec((1, t, w), lambda bi, i: (bi, 0, 0))
    seqt = lambda w: pl.BlockSpec((1, w, t), lambda bi, i: (bi, 0, 0))
    hm = pl.BlockSpec((1, IDX_HEADS, tq, IDX_DIM), lambda bi, i: (bi, 0, i, 0))
    return pl.pallas_call(
        functools.partial(_attn_prompt_kernel, topk=topk),
        grid=(b, t // tq),
        in_specs=[tile(ATTN_HEADS * LANES), hm, tile(IDX_HEADS), seqt(IDX_DIM),
                  seqt(LANES), seq(LANES), seq(LANES)],
        out_specs=tile(ATTN_WIDTH),
        out_shape=jax.ShapeDtypeStruct((b, t, ATTN_WIDTH), BF16),
        scratch_shapes=[pltpu.VMEM((t // tk, tq, tk), F32)] + [pltpu.VMEM((tq, LANES), F32)] * 4
                       + [pltpu.VMEM((tq, 1), F32), pltpu.VMEM((tq, 1), I32),
                          pltpu.VMEM((ATTN_HEADS, tq, LANES), F32), pltpu.VMEM((ATTN_HEADS, tq, LANES), F32)],
        compiler_params=pltpu.CompilerParams(dimension_semantics=("arbitrary", "arbitrary"),
                                             vmem_limit_bytes=VMEM_LIMIT),
        name="attn_prompt",
    )(qpad, qib, wi, kitb, ktb, v0, v1)


def _score_sample_kernel(pt_ref, qih_ref, qil_ref, wcol_ref, knh_ref, knl_ref, *rest):
    pages = rest[:PAGES_PER_STEP]
    scp_ref, scn_ref, kh_s, kl_s = rest[PAGES_PER_STEP:]
    c = pl.program_id(1)
    ds = qih_ref.shape[1]
    page = pages[0].shape[1]
    qh = qih_ref[...].reshape(IDX_HEADS * ds, IDX_DIM)
    ql = qil_ref[...].reshape(IDX_HEADS * ds, IDX_DIM)
    wcol = wcol_ref[0]

    def combine(s):
        r = jnp.maximum(s, 0.0) * wcol
        return jnp.sum(r.reshape(IDX_HEADS, ds, r.shape[1]), axis=0)

    for pg in range(PAGES_PER_STEP):
        kh, kl = _split(pages[pg][...])
        kh_s[:, pg * page:(pg + 1) * page] = kh
        kl_s[:, pg * page:(pg + 1) * page] = kl
    kh = kh_s[...]
    scp_ref[0] = combine(_dot(qh, kh) + _dot(ql, kh) + _dot(qh, kl_s[...]))

    @pl.when(c == pl.num_programs(1) - 1)
    def _():
        knh = knh_ref[0]
        sn = combine(_dot_nt(qh, knh) + _dot_nt(ql, knh) + _dot_nt(qh, knl_ref[0]))
        tok = lax.broadcasted_iota(I32, sn.shape, 0)
        col = lax.broadcasted_iota(I32, sn.shape, 1)
        scn_ref[0] = jnp.where(col <= tok, sn, -jnp.inf)


def _select_sample_kernel(scp_ref, scn_ref, thr_ref, cidx_ref, need_ref, cut_ref, *, topk, blk_w):
    rows, past = scp_ref.shape

    def sources(rs):
        return [(lambda j: scp_ref[rs, pl.ds(pl.multiple_of(j * blk_w, blk_w), blk_w)], past // blk_w, blk_w, 0),
                (lambda j: scn_ref[rs, :], 1, scn_ref.shape[1], past)]

    src = sources(slice(None))
    lane_red = lambda fn, op, init, red: red(_row_reduce(src, rows, fn, op, init), axis=1, keepdims=True)
    vmax = lane_red(lambda blk, col: blk, jnp.maximum, -jnp.inf, jnp.max)
    vmin = lane_red(lambda blk, col: jnp.where(blk == -jnp.inf, jnp.inf, blk), jnp.minimum, jnp.inf, jnp.min)
    nvis = _count(src, rows, lambda blk, col: blk > -jnp.inf)
    cpos = _count(src, rows, lambda blk, col: blk > 0.0)
    cnn = _count(src, rows, lambda blk, col: blk >= 0.0)
    thr, cidx, need, cut = _topk_select(src, rows, topk, vmin, vmax, nvis, cpos, cnn)
    thr_ref[...] = thr
    cidx_ref[...] = cidx
    need_ref[...] = need
    cut_ref[...] = cut
    _tie_cutoffs(lambda r0: sources(pl.ds(r0, 8)), rows, past + scn_ref.shape[1],
                 thr_ref, need_ref, cut_ref, cidx_ref)


def _attn_sample_kernel(pt_ref, qpad_ref, scp_ref, scn_ref, thr_ref, cidx_ref, knew_ref, vnew_ref, *rest,
                        past_len):
    kpages = rest[:PAGES_PER_STEP]
    vpages = rest[PAGES_PER_STEP:2 * PAGES_PER_STEP]
    o_ref, kt_s, vt_s, m_ref, l_ref, acc_ref = rest[2 * PAGES_PER_STEP:]
    c = pl.program_id(1)
    ds = qpad_ref.shape[0]
    page = kpages[0].shape[1]
    thr = thr_ref[...]
    cidx = cidx_ref[...]

    @pl.when(c == 0)
    def _():
        m_ref[...] = jnp.full(m_ref.shape, NEG, F32)
        l_ref[...] = jnp.zeros(l_ref.shape, F32)
        acc_ref[...] = jnp.zeros(acc_ref.shape, F32)

    q_all = jnp.concatenate([qpad_ref[:, hh * LANES:(hh + 1) * LANES] for hh in range(ATTN_HEADS)], axis=0)

    def attend(sc, col, s_of_q, pv):
        bias1 = jnp.where(_selected(sc, col, thr, cidx), 0.0, NEG)
        s = s_of_q + jnp.concatenate([bias1] * ATTN_HEADS, axis=0)
        m_old = m_ref[...]
        m_new = jnp.maximum(m_old, jnp.max(s, axis=1, keepdims=True))
        alpha = jnp.exp2(m_old - m_new)
        p = jnp.exp2(s - m_new)
        l_ref[...] = alpha * l_ref[...] + jnp.sum(p, axis=1, keepdims=True)
        acc_ref[...] = alpha * acc_ref[...] + pv(p.astype(BF16))
        m_ref[...] = m_new

    for pg in range(PAGES_PER_STEP):
        kt_s[:, pg * page:(pg + 1) * page] = kpages[pg][...].astype(BF16)
        vt_s[:, pg * page:(pg + 1) * page] = vpages[pg][...].astype(BF16)
    width = PAGES_PER_STEP * page
    col = c * width + lax.broadcasted_iota(I32, (ds, width), 1)
    attend(scp_ref[0], col, _dot(q_all, kt_s[...]), lambda p: _dot_nt(p, vt_s[...]))

    @pl.when(c == pl.num_programs(1) - 1)
    def _():
        sc = scn_ref[0]
        ncol = past_len + lax.broadcasted_iota(I32, sc.shape, 1)
        attend(sc, ncol, _dot_nt(q_all, knew_ref[0]), lambda p: _dot(p, vnew_ref[0]))
        o = acc_ref[...] / l_ref[...]
        outs = []
        for hh in range(ATTN_HEADS):
            g = hh // KV_GROUP
            outs.append(o[hh * ds:(hh + 1) * ds, g * HEAD_DIM:(g + 1) * HEAD_DIM])
        o_ref[...] = jnp.concatenate(outs, axis=1).astype(BF16)


def _attn_sample(qpad, qih, qil, wi, kih, kil, kb, vb, pool_kt, pool_vt, pool_ikt, page_table, topk, db):
    n = qpad.shape[0]
    ds = n // db
    n_pool, _, page = pool_ikt.shape
    n_pages = page_table.shape[1]
    past_len = n_pages * page
    pps = PAGES_PER_STEP
    n_chunks = n_pages // pps
    assert n_pages % pps == 0 and ds <= LANES

    wcol = wi.reshape(db, ds, IDX_HEADS).transpose(0, 2, 1).reshape(db, IDX_HEADS * ds, 1)
    padk = lambda a: jnp.pad(a.reshape(db, ds, a.shape[1]), ((0, 0), (0, LANES - ds), (0, 0)))
    knh, knl, knew, vnew = padk(kih), padk(kil), padk(kb), padk(vb)

    def page_spec(w, pg):
        return pl.BlockSpec((None, w, page), lambda b, c, pt: (pt[b, c * pps + pg], 0, 0))

    hm = pl.BlockSpec((IDX_HEADS, ds, IDX_DIM), lambda b, c, pt: (0, b, 0))
    seq3 = lambda r, w: pl.BlockSpec((1, r, w), lambda b, c, pt: (b, 0, 0))
    chunk = pl.BlockSpec((1, ds, pps * page), lambda b, c, pt: (b, 0, c))
    scp, scn = pl.pallas_call(
        _score_sample_kernel,
        grid_spec=pltpu.PrefetchScalarGridSpec(
            num_scalar_prefetch=1, grid=(db, n_chunks),
            in_specs=[hm, hm, seq3(IDX_HEADS * ds, 1), seq3(LANES, IDX_DIM), seq3(LANES, IDX_DIM)]
                     + [page_spec(IDX_DIM, pg) for pg in range(pps)],
            out_specs=[chunk, seq3(ds, LANES)],
            scratch_shapes=[pltpu.VMEM((IDX_DIM, pps * page), BF16), pltpu.VMEM((IDX_DIM, pps * page), BF16)]),
        out_shape=[jax.ShapeDtypeStruct((db, ds, past_len), F32), jax.ShapeDtypeStruct((db, ds, LANES), F32)],
        compiler_params=pltpu.CompilerParams(dimension_semantics=("arbitrary", "arbitrary"),
                                             vmem_limit_bytes=VMEM_LIMIT),
        name="score_sample",
    )(page_table, qih, qil, wcol, knh, knl, *([pool_ikt] * pps))

    rows = min(SEL_ROWS, n)
    blk_w = 512
    thr, cidx = pl.pallas_call(
        functools.partial(_select_sample_kernel, topk=topk, blk_w=blk_w),
        grid=(n // rows,),
        in_specs=[pl.BlockSpec((rows, past_len), lambda r: (r, 0)), pl.BlockSpec((rows, LANES), lambda r: (r, 0))],
        out_specs=[pl.BlockSpec((rows, 1), lambda r: (r, 0)), pl.BlockSpec((rows, 1), lambda r: (r, 0))],
        out_shape=[jax.ShapeDtypeStruct((n, 1), F32), jax.ShapeDtypeStruct((n, 1), I32)],
        scratch_shapes=[pltpu.VMEM((rows, 1), F32), pltpu.VMEM((rows, 1), F32)],
        compiler_params=pltpu.CompilerParams(dimension_semantics=("arbitrary",), vmem_limit_bytes=VMEM_LIMIT),
        name="select_sample",
    )(scp.reshape(n, past_len), scn.reshape(n, LANES))

    tokrows = lambda w: pl.BlockSpec((ds, w), lambda b, c, pt: (b, 0))
    nrow = ATTN_HEADS * ds
    return pl.pallas_call(
        functools.partial(_attn_sample_kernel, past_len=past_len),
        grid_spec=pltpu.PrefetchScalarGridSpec(
            num_scalar_prefetch=1, grid=(db, n_chunks),
            in_specs=[tokrows(ATTN_HEADS * LANES), chunk, seq3(ds, LANES),
                      tokrows(1), tokrows(1), seq3(LANES, LANES), seq3(LANES, LANES)]
                     + [page_spec(LANES, pg) for pg in range(pps)] * 2,
            out_specs=tokrows(ATTN_WIDTH),
            scratch_shapes=[pltpu.VMEM((LANES, pps * page), BF16), pltpu.VMEM((LANES, pps * page), BF16),
                            pltpu.VMEM((nrow, 1), F32), pltpu.VMEM((nrow, 1), F32),
                            pltpu.VMEM((nrow, LANES), F32)]),
        out_shape=jax.ShapeDtypeStruct((n, ATTN_WIDTH), BF16),
        compiler_params=pltpu.CompilerParams(dimension_semantics=("arbitrary", "arbitrary"),
                                             vmem_limit_bytes=VMEM_LIMIT),
        name="attn_sample",
    )(page_table, qpad, scp, scn, thr, cidx, knew, vnew, *([pool_kt] * pps), *([pool_vt] * pps))


def _mix_mlp_kernel(a_ref, c_ref, x_ref, woa_ref, woc_ref, g1_ref, g2_ref, wup_ref, wdn_ref, g3_ref, y_ref):
    m = _dot(a_ref[...], woa_ref[...]) + _dot(c_ref[...], woc_ref[...])
    x1 = x_ref[...] + _rms(m) * g1_ref[...]
    h = (_rms(x1) * g2_ref[...]).astype(BF16)
    dff = wup_ref.shape[1]
    f = jnp.zeros(x1.shape, F32)
    for cc in range(dff // FF_CHUNK):
        cs = slice(cc * FF_CHUNK, (cc + 1) * FF_CHUNK)
        up = jnp.maximum(_dot(h, wup_ref[:, cs]), 0.0)
        f = f + _dot((up * up).astype(BF16), wdn_ref[cs, :])
    y_ref[...] = x1 + _rms(f) * g3_ref[...]


def _mix_mlp(a, c, x, woa, woc, g1, g2, wup, wdn, g3):
    n, d = x.shape
    tm = min(MLP_TILE, n)
    full = lambda w: pl.BlockSpec(w.shape, lambda i: (0,) * w.ndim)
    tok = lambda w: pl.BlockSpec((tm, w), lambda i: (i, 0))
    return pl.pallas_call(
        _mix_mlp_kernel,
        grid=(n // tm,),
        in_specs=[tok(a.shape[1]), tok(c.shape[1]), tok(d), full(woa), full(woc), full(g1), full(g2),
                  full(wup), full(wdn), full(g3)],
        out_specs=tok(d),
        out_shape=jax.ShapeDtypeStruct((n, d), F32),
        compiler_params=pltpu.CompilerParams(dimension_semantics=("arbitrary",), vmem_limit_bytes=VMEM_LIMIT),
        name="mix_mlp",
    )(a, c, x, woa, woc, g1, g2, wup, wdn, g3)


def _rope_tables(pos):
    rot = HEAD_DIM // 4
    half = rot // 2
    inv_freq = jnp.power(ROPE_THETA, -jnp.arange(half, dtype=F32) * (2.0 / rot))
    ang = pos.astype(F32)[:, None] * inv_freq[None, :]
    cos, sin = jnp.cos(ang), jnp.sin(ang)
    n = pos.shape[0]
    zeros = lambda w: jnp.zeros((n, w), F32)
    reps = LANES // HEAD_DIM
    rc = jnp.concatenate([cos, cos, jnp.ones((n, HEAD_DIM - rot), F32)] * reps, axis=1)
    ra = jnp.concatenate([zeros(half), sin, zeros(HEAD_DIM - rot)] * reps, axis=1)
    rb = jnp.concatenate([-sin, zeros(HEAD_DIM - half)] * reps, axis=1)
    return rc, ra, rb


def _prep_w_in(w):
    d = w.shape[0]
    sizes = (ATTN_WIDTH, N_KV_HEADS * HEAD_DIM, N_KV_HEADS * HEAD_DIM, IDX_HEADS * IDX_DIM, IDX_DIM, IDX_HEADS)
    offs = np.cumsum((0,) + sizes)
    wq, wk, wv, wqi, wki, wwi = [w[:, offs[i]:offs[i + 1]] for i in range(6)]
    wu = w[:, offs[6]:]
    zpad = jnp.zeros((d, HEAD_DIM), w.dtype)
    qcols = []
    for hh in range(ATTN_HEADS):
        piece = wq[:, hh * HEAD_DIM:(hh + 1) * HEAD_DIM]
        qcols += [piece, zpad] if hh // KV_GROUP == 0 else [zpad, piece]
    wm = jnp.concatenate(qcols + [wk, wv, wu], axis=1).astype(BF16)
    widx = jnp.concatenate([wqi, wki, wwi, jnp.zeros((d, LANES - IDX_DIM - IDX_HEADS), w.dtype)], axis=1)
    wih, wil = _split(widx)
    return wm, wih, wil


def kernel(x_prompt, x_sample, cache_k, cache_v, cache_idx_k, state_conv, page_table, norm_mix_pre, w_in,
           conv_dw_w, conv_dw_b, conv_ln_g, conv_ln_b, w_out, norm_mix_post, norm_mlp_pre, w_up, w_down,
           norm_mlp_post):
    depth = w_in.shape[0]
    b, t, d = x_prompt.shape
    db, ds, _ = x_sample.shape
    n_pool, page = cache_k.shape[1], cache_k.shape[2]
    past_len = page_table.shape[1] * page
    topk_p = min(INDEX_TOPK, t // 4)
    topk_s = min(INDEX_TOPK, (past_len + ds) // 4)
    hist = CONV_WIDTH - 1

    rcp, rap, rbp = _rope_tables(jnp.arange(t, dtype=I32))
    rcs, ras, rbs = [jnp.tile(a, (db, 1)) for a in _rope_tables(past_len + jnp.arange(ds, dtype=I32))]

    xp = x_prompt
    xs = x_sample.reshape(db * ds, d)
    outs = [[] for _ in range(8)]
    row = lambda a: a.reshape(1, -1)
    for l in range(depth):
        wm, wih, wil = _prep_w_in(w_in[l])
        conv_args = (conv_dw_w[l], row(conv_dw_b[l]), row(conv_ln_g[l]), row(conv_ln_b[l]))
        woa = w_out[l, :ATTN_WIDTH].astype(BF16)
        woc = w_out[l, ATTN_WIDTH:].astype(BF16)
        mlp_args = (woa, woc, row(norm_mix_post[l]), row(norm_mlp_pre[l]), w_up[l].astype(BF16),
                    w_down[l].astype(BF16), row(norm_mlp_post[l]))

        (qpad, kt32, vt32, ktb, v0, v1, kit32, kitb, qib, wi, co, ctail) = _inproj_prompt(
            xp, row(norm_mix_pre[l]), wm, wih, wil, rcp, rap, rbp, *conv_args)
        ao = _attn_prompt(qpad, qib, wi, kitb, ktb, v0, v1, topk_p)
        xp = _mix_mlp(ao.reshape(b * t, -1), co.reshape(b * t, -1), xp.reshape(b * t, d), *mlp_args).reshape(b, t, d)
        unt = lambda a: a.reshape(b, N_KV_HEADS, HEAD_DIM, t).transpose(0, 3, 1, 2)
        outs[0].append(unt(kt32))
        outs[1].append(unt(vt32))
        outs[2].append(kit32.transpose(0, 2, 1))
        outs[3].append(ctail[:, HALO - hist:])

        pool_kt = cache_k[l].transpose(0, 2, 3, 1).reshape(n_pool, N_KV_HEADS * HEAD_DIM, page)
        pool_vt = cache_v[l].transpose(0, 2, 3, 1).reshape(n_pool, N_KV_HEADS * HEAD_DIM, page)
        pool_ikt = cache_idx_k[l].transpose(0, 2, 1)
        (qpad, k32, v32, kb, vb, ki32, kih, kil, qih, qil, wi, co, cnew) = _inproj_sample(
            xs, row(norm_mix_pre[l]), wm, wih, wil, rcs, ras, rbs, *conv_args, state_conv[l].transpose(1, 0, 2))
        ao = _attn_sample(qpad, qih, qil, wi, kih, kil, kb, vb, pool_kt, pool_vt, pool_ikt,
                          page_table, topk_s, db)
        xs = _mix_mlp(ao, co, xs, *mlp_args)
        outs[4].append(k32.reshape(db, ds, N_KV_HEADS, HEAD_DIM))
        outs[5].append(v32.reshape(db, ds, N_KV_HEADS, HEAD_DIM))
        outs[6].append(ki32.reshape(db, ds, IDX_DIM))
        outs[7].append(cnew.transpose(1, 0, 2))

    st = [jnp.stack(o, 0) for o in outs]
    return (xp, xs.reshape(db, ds, d), st[0], st[1], st[2], st[3], st[4], st[5], st[6], st[7])
```

```python
import functools
from typing import Any, Callable, NamedTuple

import numpy as np
import jax
import jax.numpy as jnp
from jax import lax
from jax.experimental import pallas as pl
from jax.experimental.pallas import tpu as pltpu

F32 = jnp.float32
BF16 = jnp.bfloat16
I32 = jnp.int32

ATTN_HEADS = 8
HEAD_DIM = 64
ATTN_WIDTH = ATTN_HEADS * HEAD_DIM
N_KV_HEADS = 2
KV_GROUP = ATTN_HEADS // N_KV_HEADS
IDX_HEADS = 8
IDX_DIM = 64
INDEX_TOPK = 256
ROPE_THETA = 500000.0
CONV_WIDTH = 31
RMS_EPS = 1e-6
LN_EPS = 1e-5
SCALE = 0.125
assert HEAD_DIM ** -0.5 == SCALE and IDX_DIM ** -0.5 == SCALE
assert N_KV_HEADS * HEAD_DIM == 128

LANES = 128
LOG2E = 1.4426950408889634
NEG = -(2.0 ** 100)
HALO = 32
TOK_TILE = 256
Q_TILE = 256
K_BLOCK = 512
SEL_ROWS = 128
FOLD_CHAINS = 4
BISECT_STEPS = 28
MLP_TILE = 256
FF_CHUNK = 1024
PAGES_PER_STEP = 16
VMEM_LIMIT = 56 * 1024 * 1024

KEY_NEG_INF = -2139095041
KEY_NAN_HI = 2139095041
INT_MAX = 2147483647


def _dot(a, b):
    return jnp.dot(a, b, preferred_element_type=F32)


def _dot_nt(a, b):
    return lax.dot_general(a, b, (((1,), (1,)), ((), ())), preferred_element_type=F32)


def _split(x):
    hi = x.astype(BF16)
    lo = (x - hi.astype(F32)).astype(BF16)
    return hi, lo


def _rms(x):
    return x * lax.rsqrt(jnp.mean(x * x, axis=-1, keepdims=True) + RMS_EPS)


def _sigmoid(x):
    return 1.0 / (1.0 + jnp.exp(-x))


def _inproj(x, g_pre, wm_ref, wih_ref, wil_ref, rc, ra, rb):
    h = _rms(x) * g_pre
    h_hi, h_lo = _split(h)
    zm = _dot(h_hi, wm_ref[...])
    wih = wih_ref[...]
    zi = _dot(h_hi, wih) + _dot(h_lo, wih) + _dot(h_hi, wil_ref[...])

    def rope(c):
        return c * rc + pltpu.roll(c, 8, 1) * ra + pltpu.roll(c, LANES - 8, 1) * rb

    nq = ATTN_HEADS * LANES
    q = [rope(zm[:, c * LANES:(c + 1) * LANES]) * (SCALE * LOG2E) for c in range(ATTN_HEADS)]
    k = rope(zm[:, nq:nq + LANES])
    v = zm[:, nq + LANES:nq + 2 * LANES]
    cch = zm.shape[1] - nq - 2 * LANES
    ua = zm[:, nq + 2 * LANES:nq + 2 * LANES + cch // 2]
    ub = zm[:, nq + 2 * LANES + cch // 2:]
    u = ua * _sigmoid(ub)
    qic = [rope(zi[:, c * LANES:(c + 1) * LANES]) * SCALE for c in range(IDX_HEADS * IDX_DIM // LANES)]
    kc = zi[:, IDX_HEADS * IDX_DIM:IDX_HEADS * IDX_DIM + LANES]
    kiw = jnp.where(lax.broadcasted_iota(I32, kc.shape, 1) < IDX_DIM, rope(kc), kc)
    return dict(q=q, k=k, v=v, u=u, qic=qic, kiw=kiw)


def _conv_post(acc, lng, lnb):
    mu = jnp.mean(acc, axis=-1, keepdims=True)
    d = acc - mu
    var = jnp.mean(d * d, axis=-1, keepdims=True)
    y = d * lax.rsqrt(var + LN_EPS) * lng + lnb
    return y * _sigmoid(y)


def _inproj_prompt_kernel(x_ref, g_ref, wm_ref, wih_ref, wil_ref, rc_ref, ra_ref, rb_ref,
                          cw_ref, cb_ref, lng_ref, lnb_ref,
                          qt_ref, kt32_ref, vt32_ref, kb_ref, vto_ref, kit32_ref, kib_ref,
                          qit_ref, wit_ref, co_ref, ctail_ref, ubuf):
    i = pl.program_id(1)
    tt = x_ref.shape[1]
    p = _inproj(x_ref[0], g_ref[...], wm_ref, wih_ref, wil_ref, rc_ref[...], ra_ref[...], rb_ref[...])
    for hh in range(ATTN_HEADS):
        qt_ref[0, hh] = p["q"][hh].T.astype(BF16)
    kt32_ref[0] = p["k"].T
    kb_ref[0] = p["k"].astype(BF16)
    vt = p["v"].T
    vt32_ref[0] = vt
    ones = jnp.ones((HEAD_DIM, tt), F32)
    for g in range(N_KV_HEADS):
        vto_ref[0, g] = jnp.concatenate([vt[g * HEAD_DIM:(g + 1) * HEAD_DIM], ones], axis=0).astype(BF16)
    kiw = p["kiw"]
    kib_ref[0] = kiw[:, :IDX_DIM].astype(BF16)
    kiwt = kiw.T
    kit32_ref[0] = kiwt[:IDX_DIM]
    wit_ref[0] = kiwt[IDX_DIM:IDX_DIM + IDX_HEADS]
    for c, chunk in enumerate(p["qic"]):
        ct = chunk.T.astype(BF16)
        for half in range(LANES // IDX_DIM):
            qit_ref[0, c * (LANES // IDX_DIM) + half] = ct[half * IDX_DIM:(half + 1) * IDX_DIM]

    @pl.when(i == 0)
    def _():
        ubuf[0:HALO, :] = jnp.zeros((HALO, ubuf.shape[1]), F32)

    @pl.when(i > 0)
    def _():
        ubuf[0:HALO, :] = ubuf[tt:tt + HALO, :]

    ubuf[HALO:HALO + tt, :] = p["u"]
    ctail_ref[0] = ubuf[tt:tt + HALO, :]

    rc_rows = 64
    cb = cb_ref[...]
    lng = lng_ref[...]
    lnb = lnb_ref[...]
    off = HALO - (CONV_WIDTH - 1)
    for r in range(tt // rc_rows):
        acc = jnp.broadcast_to(cb, (rc_rows, cb.shape[1]))
        for j in range(CONV_WIDTH):
            acc = acc + ubuf[pl.ds(off + j + r * rc_rows, rc_rows), :] * cw_ref[j:j + 1, :]
        co_ref[0, r * rc_rows:(r + 1) * rc_rows, :] = _conv_post(acc, lng, lnb).astype(BF16)


def _inproj_sample_kernel(x_ref, g_ref, wm_ref, wih_ref, wil_ref, rc_ref, ra_ref, rb_ref,
                          cw_ref, cb_ref, lng_ref, lnb_ref, st_ref,
                          qpad_ref, k32_ref, v32_ref, kb_ref, vb_ref, ki32_ref, kih_ref, kil_ref,
                          qih_ref, qil_ref, wi_ref, co_ref, cnew_ref, fbuf, uscr):
    hist, db, cch = st_ref.shape
    ds = x_ref.shape[0] // db
    p = _inproj(x_ref[...], g_ref[...], wm_ref, wih_ref, wil_ref, rc_ref[...], ra_ref[...], rb_ref[...])
    for hh in range(ATTN_HEADS):
        qpad_ref[:, hh * LANES:(hh + 1) * LANES] = p["q"][hh].astype(BF16)
    k32_ref[...] = p["k"]
    v32_ref[...] = p["v"]
    kb_ref[...] = p["k"].astype(BF16)
    vb_ref[...] = p["v"].astype(BF16)
    ki = p["kiw"][:, :IDX_DIM]
    ki32_ref[...] = ki
    kih, kil = _split(ki)
    kih_ref[...] = kih
    kil_ref[...] = kil
    wi_ref[...] = p["kiw"][:, IDX_DIM:IDX_DIM + IDX_HEADS]
    per_chunk = LANES // IDX_DIM
    for hh in range(IDX_HEADS):
        half = hh % per_chunk
        hi, lo = _split(p["qic"][hh // per_chunk][:, half * IDX_DIM:(half + 1) * IDX_DIM])
        qih_ref[hh] = hi
        qil_ref[hh] = lo

    nch = cch // LANES
    u = p["u"]
    for cc in range(nch):
        uscr[cc] = u[:, cc * LANES:(cc + 1) * LANES]
    fbuf[0:hist] = st_ref[...]
    for t in range(ds):
        for cc in range(nch):
            fbuf[hist + t, :, cc * LANES:(cc + 1) * LANES] = uscr[cc, pl.ds(t, db, stride=ds), :]
    acc = jnp.broadcast_to(cb_ref[...].reshape(1, 1, cch), (ds, db, cch))
    for j in range(CONV_WIDTH):
        acc = acc + fbuf[j:j + ds] * cw_ref[j:j + 1, :].reshape(1, 1, cch)
    y = _conv_post(acc.reshape(ds * db, cch), lng_ref[...], lnb_ref[...])
    for t in range(ds):
        for cc in range(nch):
            uscr[cc, pl.ds(t, db, stride=ds), :] = y[t * db:(t + 1) * db, cc * LANES:(cc + 1) * LANES]
    for cc in range(nch):
        co_ref[:, cc * LANES:(cc + 1) * LANES] = uscr[cc].astype(BF16)
    cnew_ref[...] = fbuf[ds:ds + hist]


def _inproj_prompt(x, g_pre, wm, wih, wil, rc, ra, rb, cw, cb, lng, lnb):
    b, t, d = x.shape
    tt = TOK_TILE
    cch = cw.shape[1]
    full = lambda a: pl.BlockSpec(a.shape, lambda bi, i: (0,) * a.ndim)
    tok = lambda w: pl.BlockSpec((1, tt, w), lambda bi, i: (bi, i, 0))
    feat = lambda w: pl.BlockSpec((1, w, tt), lambda bi, i: (bi, 0, i))
    tab = pl.BlockSpec((tt, LANES), lambda bi, i: (i, 0))
    hfeat = lambda n, w: pl.BlockSpec((1, n, w, tt), lambda bi, i: (bi, 0, 0, i))
    sds = jax.ShapeDtypeStruct
    out_shape = [sds((b, ATTN_HEADS, LANES, t), BF16), sds((b, LANES, t), F32), sds((b, LANES, t), F32),
                 sds((b, t, LANES), BF16), sds((b, N_KV_HEADS, LANES, t), BF16),
                 sds((b, IDX_DIM, t), F32), sds((b, t, IDX_DIM), BF16),
                 sds((b, IDX_HEADS, IDX_DIM, t), BF16),
                 sds((b, IDX_HEADS, t), F32), sds((b, t, cch), BF16), sds((b, HALO, cch), F32)]
    out_specs = [hfeat(ATTN_HEADS, LANES), feat(LANES), feat(LANES), tok(LANES), hfeat(N_KV_HEADS, LANES),
                 feat(IDX_DIM), tok(IDX_DIM), hfeat(IDX_HEADS, IDX_DIM), feat(IDX_HEADS), tok(cch),
                 pl.BlockSpec((1, HALO, cch), lambda bi, i: (bi, 0, 0))]
    return pl.pallas_call(
        _inproj_prompt_kernel,
        grid=(b, t // tt),
        in_specs=[tok(d), full(g_pre), full(wm), full(wih), full(wil), tab, tab, tab,
                  full(cw), full(cb), full(lng), full(lnb)],
        out_specs=out_specs,
        out_shape=out_shape,
        scratch_shapes=[pltpu.VMEM((tt + HALO, cch), F32)],
        compiler_params=pltpu.CompilerParams(dimension_semantics=("arbitrary", "arbitrary"),
                                             vmem_limit_bytes=VMEM_LIMIT),
        name="inproj_prompt",
    )(x, g_pre, wm, wih, wil, rc, ra, rb, cw, cb, lng, lnb)


def _inproj_sample(x, g_pre, wm, wih, wil, rc, ra, rb, cw, cb, lng, lnb, state):
    n, d = x.shape
    hist, db, cch = state.shape
    ds = n // db
    sds = jax.ShapeDtypeStruct
    out_shape = [sds((n, ATTN_HEADS * LANES), BF16), sds((n, LANES), F32), sds((n, LANES), F32),
                 sds((n, LANES), BF16), sds((n, LANES), BF16),
                 sds((n, IDX_DIM), F32), sds((n, IDX_DIM), BF16), sds((n, IDX_DIM), BF16),
                 sds((IDX_HEADS, n, IDX_DIM), BF16), sds((IDX_HEADS, n, IDX_DIM), BF16),
                 sds((n, IDX_HEADS), F32), sds((n, cch), BF16), sds((hist, db, cch), F32)]
    return pl.pallas_call(
        _inproj_sample_kernel,
        out_shape=out_shape,
        scratch_shapes=[pltpu.VMEM((hist + ds, db, cch), F32), pltpu.VMEM((cch // LANES, n, LANES), F32)],
        compiler_params=pltpu.CompilerParams(vmem_limit_bytes=VMEM_LIMIT),
        name="inproj_sample",
    )(x, g_pre, wm, wih, wil, rc, ra, rb, cw, cb, lng, lnb, state)


def _key_to_f32(key):
    bits = jnp.where(key < 0, key ^ INT_MAX, key)
    return lax.bitcast_convert_type(bits, F32)


def _row_reduce(sources, rows, fn, op, init):
    c = jnp.full((rows, LANES), init, F32)
    for get, nblk, width, col0 in sources:
        def body(j, c, get=get, width=width, col0=col0):
            blk = get(j)
            col = col0 + j * width + lax.broadcasted_iota(I32, blk.shape, 1)
            val = fn(blk, col)
            for cc in range(width // LANES):
                c = op(c, val[:, cc * LANES:(cc + 1) * LANES])
            return c
        c = lax.fori_loop(0, nblk, body, c)
    return c


def _fold_rows(x, op, part):
    groups = [x[r:r + part] for r in range(0, x.shape[0], part)]
    parts = groups[:FOLD_CHAINS]
    for k, g in enumerate(groups[FOLD_CHAINS:]):
        parts[k % FOLD_CHAINS] = op(parts[k % FOLD_CHAINS], g)
    while len(parts) > 1:
        parts = [op(parts[k], parts[k + 1]) if k + 1 < len(parts) else parts[k] for k in range(0, len(parts), 2)]
    return parts[0]


class _KeyMajor(NamedTuple):
    get: Callable
    nblk: Any
    q: int


def _vshape(sources, rows):
    return (1, sources.q) if isinstance(sources, _KeyMajor) else (rows, 1)


def _scan(sources, rows, fn, op, red, init):
    if not isinstance(sources, _KeyMajor):
        return red(_row_reduce(sources, rows, fn, op, init), axis=1, keepdims=True)
    def body(j, c):
        for blk, key0 in sources.get(j):
            key = key0 + lax.broadcasted_iota(I32, blk.shape, 0)
            c = op(c, _fold_rows(fn(blk, key), op, 8))
        return c

    c = lax.fori_loop(0, sources.nblk, body, jnp.full((8, sources.q), init, F32))
    return red(c, axis=0, keepdims=True)


def _count(sources, rows, pred):
    return _scan(sources, rows, lambda blk, col: jnp.where(pred(blk, col), 1.0, 0.0), jnp.add, jnp.sum, 0.0)


def _topk_select(sources, rows, topk, vmin, vmax, nvis, cpos, cnn):
    kf = float(topk)
    few = nvis <= kf
    zero_top = (cpos <= kf) & (cnn >= kf) & jnp.logical_not(few)
    up = cpos > kf

    def unsettled(carry):
        return (carry[0] < BISECT_STEPS) & (jnp.min(carry[4]) < 0.5)

    def step(carry):
        it, lo, hi, thr, done = carry
        mid = 0.5 * lo + 0.5 * hi
        c = _count(sources, rows, lambda blk, col: blk > mid)
        hit = (c == kf) & (done < 0.5)
        return (it + 1, jnp.where(c > kf, mid, lo), jnp.where(c < kf, mid, hi),
                jnp.where(hit, mid, thr), jnp.where(hit, 1.0, done))

    carry0 = (jnp.int32(0), jnp.where(up, 0.0, vmin), jnp.where(up, vmax, 0.0),
              jnp.where(few, -jnp.inf, 0.0), jnp.where(few | zero_top, 1.0, 0.0))
    _, _, _, thr, done = lax.while_loop(unsettled, step, carry0)
    settled = done > 0.5
    zero_need = kf - cpos
    cidx = jnp.where(zero_top & (zero_need > 0.0), INT_MAX, -1)
    need = jnp.where(zero_top, zero_need, 0.0)
    cut = jnp.where(zero_top & (cnn > kf) & (zero_need > 0.0), 1.0, 0.0)

    def exact(_):
        thr_e, cnt_e = _kth_largest_exact(sources, rows, topk)
        need_e = kf - _count(sources, rows, lambda blk, col: blk > thr_e)
        fin = thr_e > -jnp.inf
        return (jnp.where(settled, thr, thr_e), jnp.where(settled, cidx, jnp.where(fin, INT_MAX, -1)),
                jnp.where(settled, need, need_e), jnp.where(settled, cut, jnp.where(fin & (cnt_e > kf), 1.0, 0.0)))

    return lax.cond(jnp.min(done) < 0.5, exact, lambda _: (thr, cidx, need, cut), 0)


def _tie_cutoffs(slab_sources, nrows, ncols_max, thr_ref, need_ref, cut_ref, cidx_ref):
    n_it = int(np.ceil(np.log2(ncols_max + 1))) + 1

    def slab(sl, _):
        rs = pl.ds(pl.multiple_of(sl * 8, 8), 8)
        cut = cut_ref[rs, :] > 0.5

        @pl.when(jnp.max(cut_ref[rs, :]) > 0.5)
        def _():
            thr = thr_ref[rs, :]
            need = need_ref[rs, :]
            src = slab_sources(pl.multiple_of(sl * 8, 8))

            def bis_c(_, carry):
                lo_c, hi_c = carry
                mid = (lo_c + hi_c) >> 1
                ge = _count(src, 8, lambda blk, col: (blk == thr) & (col <= mid)) >= need
                return jnp.where(ge, lo_c, mid), jnp.where(ge, mid, hi_c)

            _, hi_c = lax.fori_loop(0, n_it, bis_c, (jnp.full((8, 1), -1, I32),
                                                     jnp.full((8, 1), ncols_max - 1, I32)))
            cidx_ref[rs, :] = jnp.where(cut, hi_c, cidx_ref[rs, :])
        return 0

    lax.fori_loop(0, nrows // 8, slab, 0)


def _drop_excess_ties(get_blk, put_blk, nblk, width, q, thr, need, cut):
    tri = jnp.where(lax.broadcasted_iota(I32, (width, width), 1) <= lax.broadcasted_iota(I32, (width, width), 0),
                    1.0, 0.0).astype(BF16)
    marked = cut > 0.5

    def body(j, before):
        blk = get_blk(j)
        tied = (blk == thr) & marked
        rank = _dot(tri, jnp.where(tied, 1.0, 0.0).astype(BF16))
        put_blk(j, jnp.where(tied & (before + rank > need), -jnp.inf, blk))
        return before + jnp.max(rank, axis=0, keepdims=True)

    lax.fori_loop(0, nblk, body, jnp.zeros((1, q), F32))


def _kth_largest_exact(sources, rows, topk):
    kf = float(topk)

    def bis(_, carry):
        lo, hi = carry
        mid = (lo >> 1) + (hi >> 1) + (lo & hi & 1)
        thr = _key_to_f32(mid)
        ge = _count(sources, rows, lambda blk, col: blk >= thr) >= kf
        return jnp.where(ge, mid, lo), jnp.where(ge, hi, mid)

    lo0 = jnp.full(_vshape(sources, rows), KEY_NEG_INF, I32)
    hi0 = jnp.full(_vshape(sources, rows), KEY_NAN_HI, I32)
    _, hi = lax.fori_loop(0, 32, bis, (lo0, hi0))

    def below(upper):
        m = _scan(sources, rows, lambda blk, col: jnp.where(blk < upper, blk, -jnp.inf),
                  jnp.maximum, jnp.max, -jnp.inf)
        return m, _count(sources, rows, lambda blk, col: blk >= m)

    def short(carry):
        return jnp.max(jnp.where(carry[2] < kf, 1.0, 0.0)) > 0.0

    def lower(carry):
        upper, m, c = carry
        upper = jnp.where(c < kf, m, upper)
        m, c = below(upper)
        return upper, m, c

    upper0 = _key_to_f32(hi)
    m0, c0 = below(upper0)
    _, thr, cnt = lax.while_loop(short, lower, (upper0, m0, c0))
    return thr, cnt


def _selected(blk, col, thr, cidx):
    return (blk > thr) | ((blk == thr) & (col <= cidx))


def _attn_prompt_kernel(qt_ref, qit_ref, wit_ref, kib_ref, kb_ref, vto_ref, o_ref,
                        sc_ref, vmx_ref, vmn_ref, cp_ref, cn_ref, thr_ref, cidx_ref, m_ref, acc_ref, *, topk):
    i = pl.program_id(1)
    tq = qt_ref.shape[3]
    tk = sc_ref.shape[1]
    nkb = (i * tq + tq + tk - 1) // tk
    q_pos = i * tq + lax.broadcasted_iota(I32, (tk, tq), 1)
    key_of = lambda j: j * tk + lax.broadcasted_iota(I32, (tk, tq), 0)
    wit = wit_ref[0]

    vmx_ref[...] = jnp.full(vmx_ref.shape, -jnp.inf, F32)
    vmn_ref[...] = jnp.full(vmn_ref.shape, jnp.inf, F32)
    cp_ref[...] = jnp.zeros(cp_ref.shape, F32)
    cn_ref[...] = jnp.zeros(cn_ref.shape, F32)

    def score_block(j, _):
        kib = kib_ref[0, pl.ds(pl.multiple_of(j * tk, tk), tk), :]
        acc = jnp.zeros((tk, tq), F32)
        for hh in range(IDX_HEADS):
            acc = acc + jnp.maximum(_dot(kib, qit_ref[0, hh]), 0.0) * wit[hh:hh + 1, :]
        vis = key_of(j) <= q_pos
        sc = jnp.where(vis, acc, -jnp.inf)
        sc_ref[j] = sc
        vmx_ref[...] = jnp.maximum(vmx_ref[...], _fold_rows(sc, jnp.maximum, 8))
        vmn_ref[...] = jnp.minimum(vmn_ref[...], _fold_rows(jnp.where(vis, acc, jnp.inf), jnp.minimum, 8))
        cp_ref[...] = cp_ref[...] + _fold_rows(jnp.where(sc > 0.0, 1.0, 0.0), jnp.add, 8)
        cn_ref[...] = cn_ref[...] + _fold_rows(jnp.where(sc >= 0.0, 1.0, 0.0), jnp.add, 8)
        return 0

    lax.fori_loop(0, nkb, score_block, 0)

    @pl.when(nkb < sc_ref.shape[0])
    def _():
        sc_ref[nkb] = jnp.full((tk, tq), -jnp.inf, F32)

    pairs = _KeyMajor(lambda j: [(sc_ref[2 * j], 2 * j * tk), (sc_ref[2 * j + 1], (2 * j + 1) * tk)],
                      (nkb + 1) // 2, tq)
    nvis = (i * tq + 1 + lax.broadcasted_iota(I32, (1, tq), 1)).astype(F32)
    thr, cidx, need, cut = _topk_select(
        pairs, tq, topk, jnp.min(vmn_ref[...], axis=0, keepdims=True), jnp.max(vmx_ref[...], axis=0, keepdims=True),
        nvis, jnp.sum(cp_ref[...], axis=0, keepdims=True), jnp.sum(cn_ref[...], axis=0, keepdims=True))
    thr_ref[...] = thr
    cidx_ref[...] = cidx

    @pl.when(jnp.max(cut) > 0.5)
    def _():
        def put(j, val):
            sc_ref[j] = val
        _drop_excess_ties(lambda j: sc_ref[j], put, nkb, tk, tq, thr, need, cut)

    m_ref[...] = jnp.full(m_ref.shape, NEG, F32)
    acc_ref[...] = jnp.zeros(acc_ref.shape, F32)
    thr = thr_ref[...]
    cidx = cidx_ref[...]

    def attn_block(j, _):
        ks = pl.ds(pl.multiple_of(j * tk, tk), tk)
        bias = jnp.where(_selected(sc_ref[j], key_of(j), thr, cidx), 0.0, NEG).astype(BF16)
        kb = kb_ref[0, ks, :]
        qk = lambda hh: _dot(kb, qt_ref[0, hh])
        s_next = qk(0)
        pending = None
        for hh in range(ATTN_HEADS + 1):
            if hh < ATTN_HEADS:
                s = s_next.astype(BF16) + bias
                if hh + 1 < ATTN_HEADS:
                    s_next = qk(hh + 1)
                blk_max = jnp.max(_fold_rows(s, jnp.maximum, 16).astype(F32), axis=0, keepdims=True)
                m_old = m_ref[hh]
                m_new = jnp.maximum(m_old, blk_max)
                alpha = jnp.exp2(m_old - m_new)
                p = jnp.exp2(s - m_new.astype(BF16))
                m_ref[hh] = m_new
            if pending is not None:
                ph, palpha, pp = pending
                acc_ref[ph] = palpha * acc_ref[ph] + _dot(vto_ref[0, ph // KV_GROUP, :, ks], pp)
            pending = (hh, alpha, p) if hh < ATTN_HEADS else None
        return 0

    lax.fori_loop(0, nkb, attn_block, 0)

    outs = []
    for hh in range(ATTN_HEADS):
        acc = acc_ref[hh]
        outs.append((acc[:HEAD_DIM] / acc[HEAD_DIM:]).T)
    o_ref[0] = jnp.concatenate(outs, axis=1).astype(BF16)


def _attn_prompt(qt, qit, wit, kib, kb, vto, topk):
    b, _, _, t = qt.shape
    tq = Q_TILE
    tk = K_BLOCK
    assert t % tq == 0 and t % tk == 0 and tk % tq == 0
    hfeat = lambda n, w: pl.BlockSpec((1, n, w, tq), lambda bi, i: (bi, 0, 0, i))
    seq = lambda w: pl.BlockSpec((1, t, w), lambda bi, i: (bi, 0, 0))
    return pl.pallas_call(
        functools.partial(_attn_prompt_kernel, topk=topk),
        grid=(b, t // tq),
        in_specs=[hfeat(ATTN_HEADS, LANES), hfeat(IDX_HEADS, IDX_DIM),
                  pl.BlockSpec((1, IDX_HEADS, tq), lambda bi, i: (bi, 0, i)), seq(IDX_DIM), seq(LANES),
                  pl.BlockSpec((1, N_KV_HEADS, LANES, t), lambda bi, i: (bi, 0, 0, 0))],
        out_specs=pl.BlockSpec((1, tq, ATTN_WIDTH), lambda bi, i: (bi, i, 0)),
        out_shape=jax.ShapeDtypeStruct((b, t, ATTN_WIDTH), BF16),
        scratch_shapes=[pltpu.VMEM((t // tk, tk, tq), F32)] + [pltpu.VMEM((8, tq), F32)] * 4
                       + [pltpu.VMEM((1, tq), F32), pltpu.VMEM((1, tq), I32),
                          pltpu.VMEM((ATTN_HEADS, 1, tq), F32), pltpu.VMEM((ATTN_HEADS, LANES, tq), F32)],
        compiler_params=pltpu.CompilerParams(dimension_semantics=("arbitrary", "arbitrary"),
                                             vmem_limit_bytes=VMEM_LIMIT),
        name="attn_prompt",
    )(qt, qit, wit, kib, kb, vto)


def _score_sample_kernel(pt_ref, qih_ref, qil_ref, wcol_ref, knh_ref, knl_ref, *rest):
    pages = rest[:PAGES_PER_STEP]
    scp_ref, scn_ref, kh_s, kl_s = rest[PAGES_PER_STEP:]
    c = pl.program_id(1)
    ds = qih_ref.shape[1]
    page = pages[0].shape[1]
    qh = qih_ref[...].reshape(IDX_HEADS * ds, IDX_DIM)
    ql = qil_ref[...].reshape(IDX_HEADS * ds, IDX_DIM)
    wcol = wcol_ref[0]

    def combine(s):
        r = jnp.maximum(s, 0.0) * wcol
        return jnp.sum(r.reshape(IDX_HEADS, ds, r.shape[1]), axis=0)

    for pg in range(PAGES_PER_STEP):
        kh, kl = _split(pages[pg][...])
        kh_s[:, pg * page:(pg + 1) * page] = kh
        kl_s[:, pg * page:(pg + 1) * page] = kl
    kh = kh_s[...]
    scp_ref[0] = combine(_dot(qh, kh) + _dot(ql, kh) + _dot(qh, kl_s[...]))

    @pl.when(c == pl.num_programs(1) - 1)
    def _():
        knh = knh_ref[0]
        sn = combine(_dot_nt(qh, knh) + _dot_nt(ql, knh) + _dot_nt(qh, knl_ref[0]))
        tok = lax.broadcasted_iota(I32, sn.shape, 0)
        col = lax.broadcasted_iota(I32, sn.shape, 1)
        scn_ref[0] = jnp.where(col <= tok, sn, -jnp.inf)


def _select_sample_kernel(scp_ref, scn_ref, thr_ref, cidx_ref, need_ref, cut_ref, *, topk, blk_w):
    rows, past = scp_ref.shape

    def sources(rs):
        return [(lambda j: scp_ref[rs, pl.ds(pl.multiple_of(j * blk_w, blk_w), blk_w)], past // blk_w, blk_w, 0),
                (lambda j: scn_ref[rs, :], 1, scn_ref.shape[1], past)]

    src = sources(slice(None))
    lane_red = lambda fn, op, init, red: red(_row_reduce(src, rows, fn, op, init), axis=1, keepdims=True)
    vmax = lane_red(lambda blk, col: blk, jnp.maximum, -jnp.inf, jnp.max)
    vmin = lane_red(lambda blk, col: jnp.where(blk == -jnp.inf, jnp.inf, blk), jnp.minimum, jnp.inf, jnp.min)
    nvis = _count(src, rows, lambda blk, col: blk > -jnp.inf)
    cpos = _count(src, rows, lambda blk, col: blk > 0.0)
    cnn = _count(src, rows, lambda blk, col: blk >= 0.0)
    thr, cidx, need, cut = _topk_select(src, rows, topk, vmin, vmax, nvis, cpos, cnn)
    thr_ref[...] = thr
    cidx_ref[...] = cidx
    need_ref[...] = need
    cut_ref[...] = cut
    _tie_cutoffs(lambda r0: sources(pl.ds(r0, 8)), rows, past + scn_ref.shape[1],
                 thr_ref, need_ref, cut_ref, cidx_ref)


def _attn_sample_kernel(pt_ref, qpad_ref, scp_ref, scn_ref, thr_ref, cidx_ref, knew_ref, vnew_ref, *rest,
                        past_len):
    kpages = rest[:PAGES_PER_STEP]
    vpages = rest[PAGES_PER_STEP:2 * PAGES_PER_STEP]
    o_ref, kt_s, vt_s, m_ref, l_ref, acc_ref = rest[2 * PAGES_PER_STEP:]
    c = pl.program_id(1)
    ds = qpad_ref.shape[0]
    page = kpages[0].shape[1]
    thr = thr_ref[...]
    cidx = cidx_ref[...]

    @pl.when(c == 0)
    def _():
        m_ref[...] = jnp.full(m_ref.shape, NEG, F32)
        l_ref[...] = jnp.zeros(l_ref.shape, F32)
        acc_ref[...] = jnp.zeros(acc_ref.shape, F32)

    q_all = jnp.concatenate([qpad_ref[:, hh * LANES:(hh + 1) * LANES] for hh in range(ATTN_HEADS)], axis=0)

    def attend(sc, col, s_of_q, pv):
        bias1 = jnp.where(_selected(sc, col, thr, cidx), 0.0, NEG)
        s = s_of_q + jnp.concatenate([bias1] * ATTN_HEADS, axis=0)
        m_old = m_ref[...]
        m_new = jnp.maximum(m_old, jnp.max(s, axis=1, keepdims=True))
        alpha = jnp.exp2(m_old - m_new)
        p = jnp.exp2(s - m_new)
        l_ref[...] = alpha * l_ref[...] + jnp.sum(p, axis=1, keepdims=True)
        acc_ref[...] = alpha * acc_ref[...] + pv(p.astype(BF16))
        m_ref[...] = m_new

    for pg in range(PAGES_PER_STEP):
        kt_s[:, pg * page:(pg + 1) * page] = kpages[pg][...].astype(BF16)
        vt_s[:, pg * page:(pg + 1) * page] = vpages[pg][...].astype(BF16)
    width = PAGES_PER_STEP * page
    col = c * width + lax.broadcasted_iota(I32, (ds, width), 1)
    attend(scp_ref[0], col, _dot(q_all, kt_s[...]), lambda p: _dot_nt(p, vt_s[...]))

    @pl.when(c == pl.num_programs(1) - 1)
    def _():
        sc = scn_ref[0]
        ncol = past_len + lax.broadcasted_iota(I32, sc.shape, 1)
        attend(sc, ncol, _dot_nt(q_all, knew_ref[0]), lambda p: _dot(p, vnew_ref[0]))
        o = acc_ref[...] / l_ref[...]
        outs = []
        for hh in range(ATTN_HEADS):
            g = hh // KV_GROUP
            outs.append(o[hh * ds:(hh + 1) * ds, g * HEAD_DIM:(g + 1) * HEAD_DIM])
        o_ref[...] = jnp.concatenate(outs, axis=1).astype(BF16)


def _attn_sample(qpad, qih, qil, wi, kih, kil, kb, vb, pool_kt, pool_vt, pool_ikt, page_table, topk, db):
    n = qpad.shape[0]
    ds = n // db
    n_pool, _, page = pool_ikt.shape
    n_pages = page_table.shape[1]
    past_len = n_pages * page
    pps = PAGES_PER_STEP
    n_chunks = n_pages // pps
    assert n_pages % pps == 0 and ds <= LANES

    wcol = wi.reshape(db, ds, IDX_HEADS).transpose(0, 2, 1).reshape(db, IDX_HEADS * ds, 1)
    padk = lambda a: jnp.pad(a.reshape(db, ds, a.shape[1]), ((0, 0), (0, LANES - ds), (0, 0)))
    knh, knl, knew, vnew = padk(kih), padk(kil), padk(kb), padk(vb)

    def page_spec(w, pg):
        return pl.BlockSpec((None, w, page), lambda b, c, pt: (pt[b, c * pps + pg], 0, 0))

    hm = pl.BlockSpec((IDX_HEADS, ds, IDX_DIM), lambda b, c, pt: (0, b, 0))
    seq3 = lambda r, w: pl.BlockSpec((1, r, w), lambda b, c, pt: (b, 0, 0))
    chunk = pl.BlockSpec((1, ds, pps * page), lambda b, c, pt: (b, 0, c))
    scp, scn = pl.pallas_call(
        _score_sample_kernel,
        grid_spec=pltpu.PrefetchScalarGridSpec(
            num_scalar_prefetch=1, grid=(db, n_chunks),
            in_specs=[hm, hm, seq3(IDX_HEADS * ds, 1), seq3(LANES, IDX_DIM), seq3(LANES, IDX_DIM)]
                     + [page_spec(IDX_DIM, pg) for pg in range(pps)],
            out_specs=[chunk, seq3(ds, LANES)],
            scratch_shapes=[pltpu.VMEM((IDX_DIM, pps * page), BF16), pltpu.VMEM((IDX_DIM, pps * page), BF16)]),
        out_shape=[jax.ShapeDtypeStruct((db, ds, past_len), F32), jax.ShapeDtypeStruct((db, ds, LANES), F32)],
        compiler_params=pltpu.CompilerParams(dimension_semantics=("arbitrary", "arbitrary"),
                                             vmem_limit_bytes=VMEM_LIMIT),
        name="score_sample",
    )(page_table, qih, qil, wcol, knh, knl, *([pool_ikt] * pps))

    rows = min(SEL_ROWS, n)
    blk_w = 512
    thr, cidx = pl.pallas_call(
        functools.partial(_select_sample_kernel, topk=topk, blk_w=blk_w),
        grid=(n // rows,),
        in_specs=[pl.BlockSpec((rows, past_len), lambda r: (r, 0)), pl.BlockSpec((rows, LANES), lambda r: (r, 0))],
        out_specs=[pl.BlockSpec((rows, 1), lambda r: (r, 0)), pl.BlockSpec((rows, 1), lambda r: (r, 0))],
        out_shape=[jax.ShapeDtypeStruct((n, 1), F32), jax.ShapeDtypeStruct((n, 1), I32)],
        scratch_shapes=[pltpu.VMEM((rows, 1), F32), pltpu.VMEM((rows, 1), F32)],
        compiler_params=pltpu.CompilerParams(dimension_semantics=("arbitrary",), vmem_limit_bytes=VMEM_LIMIT),
        name="select_sample",
    )(scp.reshape(n, past_len), scn.reshape(n, LANES))

    tokrows = lambda w: pl.BlockSpec((ds, w), lambda b, c, pt: (b, 0))
    nrow = ATTN_HEADS * ds
    return pl.pallas_call(
        functools.partial(_attn_sample_kernel, past_len=past_len),
        grid_spec=pltpu.PrefetchScalarGridSpec(
            num_scalar_prefetch=1, grid=(db, n_chunks),
            in_specs=[tokrows(ATTN_HEADS * LANES), chunk, seq3(ds, LANES),
                      tokrows(1), tokrows(1), seq3(LANES, LANES), seq3(LANES, LANES)]
                     + [page_spec(LANES, pg) for pg in range(pps)] * 2,
            out_specs=tokrows(ATTN_WIDTH),
            scratch_shapes=[pltpu.VMEM((LANES, pps * page), BF16), pltpu.VMEM((LANES, pps * page), BF16),
                            pltpu.VMEM((nrow, 1), F32), pltpu.VMEM((nrow, 1), F32),
                            pltpu.VMEM((nrow, LANES), F32)]),
        out_shape=jax.ShapeDtypeStruct((n, ATTN_WIDTH), BF16),
        compiler_params=pltpu.CompilerParams(dimension_semantics=("arbitrary", "arbitrary"),
                                             vmem_limit_bytes=VMEM_LIMIT),
        name="attn_sample",
    )(page_table, qpad, scp, scn, thr, cidx, knew, vnew, *([pool_kt] * pps), *([pool_vt] * pps))


def _mix_mlp_kernel(a_ref, c_ref, x_ref, woa_ref, woc_ref, g1_ref, g2_ref, wup_ref, wdn_ref, g3_ref, y_ref):
    m = _dot(a_ref[...], woa_ref[...]) + _dot(c_ref[...], woc_ref[...])
    x1 = x_ref[...] + _rms(m) * g1_ref[...]
    h = (_rms(x1) * g2_ref[...]).astype(BF16)
    dff = wup_ref.shape[1]
    f = jnp.zeros(x1.shape, F32)
    for cc in range(dff // FF_CHUNK):
        cs = slice(cc * FF_CHUNK, (cc + 1) * FF_CHUNK)
        up = jnp.maximum(_dot(h, wup_ref[:, cs]), 0.0)
        f = f + _dot((up * up).astype(BF16), wdn_ref[cs, :])
    y_ref[...] = x1 + _rms(f) * g3_ref[...]


def _mix_mlp(a, c, x, woa, woc, g1, g2, wup, wdn, g3):
    n, d = x.shape
    tm = min(MLP_TILE, n)
    full = lambda w: pl.BlockSpec(w.shape, lambda i: (0,) * w.ndim)
    tok = lambda w: pl.BlockSpec((tm, w), lambda i: (i, 0))
    return pl.pallas_call(
        _mix_mlp_kernel,
        grid=(n // tm,),
        in_specs=[tok(a.shape[1]), tok(c.shape[1]), tok(d), full(woa), full(woc), full(g1), full(g2),
                  full(wup), full(wdn), full(g3)],
        out_specs=tok(d),
        out_shape=jax.ShapeDtypeStruct((n, d), F32),
        compiler_params=pltpu.CompilerParams(dimension_semantics=("arbitrary",), vmem_limit_bytes=VMEM_LIMIT),
        name="mix_mlp",
    )(a, c, x, woa, woc, g1, g2, wup, wdn, g3)


def _rope_tables(pos):
    rot = HEAD_DIM // 4
    half = rot // 2
    inv_freq = jnp.power(ROPE_THETA, -jnp.arange(half, dtype=F32) * (2.0 / rot))
    ang = pos.astype(F32)[:, None] * inv_freq[None, :]
    cos, sin = jnp.cos(ang), jnp.sin(ang)
    n = pos.shape[0]
    zeros = lambda w: jnp.zeros((n, w), F32)
    reps = LANES // HEAD_DIM
    rc = jnp.concatenate([cos, cos, jnp.ones((n, HEAD_DIM - rot), F32)] * reps, axis=1)
    ra = jnp.concatenate([zeros(half), sin, zeros(HEAD_DIM - rot)] * reps, axis=1)
    rb = jnp.concatenate([-sin, zeros(HEAD_DIM - half)] * reps, axis=1)
    return rc, ra, rb


def _prep_w_in(w):
    d = w.shape[0]
    sizes = (ATTN_WIDTH, N_KV_HEADS * HEAD_DIM, N_KV_HEADS * HEAD_DIM, IDX_HEADS * IDX_DIM, IDX_DIM, IDX_HEADS)
    offs = np.cumsum((0,) + sizes)
    wq, wk, wv, wqi, wki, wwi = [w[:, offs[i]:offs[i + 1]] for i in range(6)]
    wu = w[:, offs[6]:]
    zpad = jnp.zeros((d, HEAD_DIM), w.dtype)
    qcols = []
    for hh in range(ATTN_HEADS):
        piece = wq[:, hh * HEAD_DIM:(hh + 1) * HEAD_DIM]
        qcols += [piece, zpad] if hh // KV_GROUP == 0 else [zpad, piece]
    wm = jnp.concatenate(qcols + [wk, wv, wu], axis=1).astype(BF16)
    widx = jnp.concatenate([wqi, wki, wwi, jnp.zeros((d, LANES - IDX_DIM - IDX_HEADS), w.dtype)], axis=1)
    wih, wil = _split(widx)
    return wm, wih, wil


def kernel(x_prompt, x_sample, cache_k, cache_v, cache_idx_k, state_conv, page_table, norm_mix_pre, w_in,
           conv_dw_w, conv_dw_b, conv_ln_g, conv_ln_b, w_out, norm_mix_post, norm_mlp_pre, w_up, w_down,
           norm_mlp_post):
    depth = w_in.shape[0]
    b, t, d = x_prompt.shape
    db, ds, _ = x_sample.shape
    n_pool, page = cache_k.shape[1], cache_k.shape[2]
    past_len = page_table.shape[1] * page
    topk_p = min(INDEX_TOPK, t // 4)
    topk_s = min(INDEX_TOPK, (past_len + ds) // 4)
    hist = CONV_WIDTH - 1

    rcp, rap, rbp = _rope_tables(jnp.arange(t, dtype=I32))
    rcs, ras, rbs = [jnp.tile(a, (db, 1)) for a in _rope_tables(past_len + jnp.arange(ds, dtype=I32))]

    xp = x_prompt
    xs = x_sample.reshape(db * ds, d)
    outs = [[] for _ in range(8)]
    row = lambda a: a.reshape(1, -1)
    for l in range(depth):
        wm, wih, wil = _prep_w_in(w_in[l])
        conv_args = (conv_dw_w[l], row(conv_dw_b[l]), row(conv_ln_g[l]), row(conv_ln_b[l]))
        woa = w_out[l, :ATTN_WIDTH].astype(BF16)
        woc = w_out[l, ATTN_WIDTH:].astype(BF16)
        mlp_args = (woa, woc, row(norm_mix_post[l]), row(norm_mlp_pre[l]), w_up[l].astype(BF16),
                    w_down[l].astype(BF16), row(norm_mlp_post[l]))

        (qt, kt32, vt32, kb, vto, kit32, kib, qit, wit, co, ctail) = _inproj_prompt(
            xp, row(norm_mix_pre[l]), wm, wih, wil, rcp, rap, rbp, *conv_args)
        ao = _attn_prompt(qt, qit, wit, kib, kb, vto, topk_p)
        xp = _mix_mlp(ao.reshape(b * t, -1), co.reshape(b * t, -1), xp.reshape(b * t, d), *mlp_args).reshape(b, t, d)
        unt = lambda a: a.reshape(b, N_KV_HEADS, HEAD_DIM, t).transpose(0, 3, 1, 2)
        outs[0].append(unt(kt32))
        outs[1].append(unt(vt32))
        outs[2].append(kit32.transpose(0, 2, 1))
        outs[3].append(ctail[:, HALO - hist:])

        pool_kt = cache_k[l].transpose(0, 2, 3, 1).reshape(n_pool, N_KV_HEADS * HEAD_DIM, page)
        pool_vt = cache_v[l].transpose(0, 2, 3, 1).reshape(n_pool, N_KV_HEADS * HEAD_DIM, page)
        pool_ikt = cache_idx_k[l].transpose(0, 2, 1)
        (qpad, k32, v32, kb, vb, ki32, kih, kil, qih, qil, wi, co, cnew) = _inproj_sample(
            xs, row(norm_mix_pre[l]), wm, wih, wil, rcs, ras, rbs, *conv_args, state_conv[l].transpose(1, 0, 2))
        ao = _attn_sample(qpad, qih, qil, wi, kih, kil, kb, vb, pool_kt, pool_vt, pool_ikt,
                          page_table, topk_s, db)
        xs = _mix_mlp(ao, co, xs, *mlp_args)
        outs[4].append(k32.reshape(db, ds, N_KV_HEADS, HEAD_DIM))
        outs[5].append(v32.reshape(db, ds, N_KV_HEADS, HEAD_DIM))
        outs[6].append(ki32.reshape(db, ds, IDX_DIM))
        outs[7].append(cnew.transpose(1, 0, 2))

    st = [jnp.stack(o, 0) for o in outs]
    return (xp, xs.reshape(db, ds, d), st[0], st[1], st[2], st[3], st[4], st[5], st[6], st[7])
```

```python
import functools
from typing import Any, Callable, NamedTuple

import numpy as np
import jax
import jax.numpy as jnp
from jax import lax
from jax.experimental import pallas as pl
from jax.experimental.pallas import tpu as pltpu

F32 = jnp.float32
BF16 = jnp.bfloat16
I32 = jnp.int32

ATTN_HEADS = 8
HEAD_DIM = 64
ATTN_WIDTH = ATTN_HEADS * HEAD_DIM
N_KV_HEADS = 2
KV_GROUP = ATTN_HEADS // N_KV_HEADS
IDX_HEADS = 8
IDX_DIM = 64
INDEX_TOPK = 256
ROPE_THETA = 500000.0
CONV_WIDTH = 31
RMS_EPS = 1e-6
LN_EPS = 1e-5
SCALE = 0.125
assert HEAD_DIM ** -0.5 == SCALE and IDX_DIM ** -0.5 == SCALE
assert N_KV_HEADS * HEAD_DIM == 128

LANES = 128
LOG2E = 1.4426950408889634
NEG = -(2.0 ** 100)
HALO = 32
TOK_TILE = 256
Q_TILE = 256
K_BLOCK = 512
SEL_ROWS = 128
FOLD_CHAINS = 4
BISECT_STEPS = 28
MLP_TILE = 256
FF_CHUNK = 1024
PAGES_PER_STEP = 32
VMEM_LIMIT = 56 * 1024 * 1024

KEY_NEG_INF = -2139095041
KEY_NAN_HI = 2139095041
INT_MAX = 2147483647


def _dot(a, b):
    return jnp.dot(a, b, preferred_element_type=F32)


def _dot_nt(a, b):
    return lax.dot_general(a, b, (((1,), (1,)), ((), ())), preferred_element_type=F32)


def _split(x):
    hi = x.astype(BF16)
    lo = (x - hi.astype(F32)).astype(BF16)
    return hi, lo


def _rms(x):
    return x * lax.rsqrt(jnp.mean(x * x, axis=-1, keepdims=True) + RMS_EPS)


def _sigmoid(x):
    return 1.0 / (1.0 + jnp.exp(-x))


def _inproj(x, g_pre, wm_ref, wih_ref, wil_ref, rc, ra, rb):
    h = _rms(x) * g_pre
    h_hi, h_lo = _split(h)
    zm = _dot(h_hi, wm_ref[...])
    wih = wih_ref[...]
    zi = _dot(h_hi, wih) + _dot(h_lo, wih) + _dot(h_hi, wil_ref[...])

    def rope(c):
        return c * rc + pltpu.roll(c, 8, 1) * ra + pltpu.roll(c, LANES - 8, 1) * rb

    nq = ATTN_HEADS * LANES
    q = [rope(zm[:, c * LANES:(c + 1) * LANES]) * (SCALE * LOG2E) for c in range(ATTN_HEADS)]
    k = rope(zm[:, nq:nq + LANES])
    v = zm[:, nq + LANES:nq + 2 * LANES]
    cch = zm.shape[1] - nq - 2 * LANES
    ua = zm[:, nq + 2 * LANES:nq + 2 * LANES + cch // 2]
    ub = zm[:, nq + 2 * LANES + cch // 2:]
    u = ua * _sigmoid(ub)
    qic = [rope(zi[:, c * LANES:(c + 1) * LANES]) * SCALE for c in range(IDX_HEADS * IDX_DIM // LANES)]
    kc = zi[:, IDX_HEADS * IDX_DIM:IDX_HEADS * IDX_DIM + LANES]
    kiw = jnp.where(lax.broadcasted_iota(I32, kc.shape, 1) < IDX_DIM, rope(kc), kc)
    return dict(q=q, k=k, v=v, u=u, qic=qic, kiw=kiw)


def _conv_post(acc, lng, lnb):
    mu = jnp.mean(acc, axis=-1, keepdims=True)
    d = acc - mu
    var = jnp.mean(d * d, axis=-1, keepdims=True)
    y = d * lax.rsqrt(var + LN_EPS) * lng + lnb
    return y * _sigmoid(y)


def _inproj_prompt_kernel(x_ref, g_ref, wm_ref, wih_ref, wil_ref, rc_ref, ra_ref, rb_ref,
                          cw_ref, cb_ref, lng_ref, lnb_ref,
                          qt_ref, kt32_ref, vt32_ref, kb_ref, vto_ref, kit32_ref, kib_ref,
                          qit_ref, wit_ref, co_ref, ctail_ref, ubuf, ush):
    i = pl.program_id(1)
    tt = x_ref.shape[1]
    p = _inproj(x_ref[0], g_ref[...], wm_ref, wih_ref, wil_ref, rc_ref[...], ra_ref[...], rb_ref[...])
    for hh in range(ATTN_HEADS):
        qt_ref[0, hh] = p["q"][hh].T.astype(BF16)
    kt32_ref[0] = p["k"].T
    kb_ref[0] = p["k"].astype(BF16)
    vt = p["v"].T
    vt32_ref[0] = vt
    ones = jnp.ones((HEAD_DIM, tt), F32)
    for g in range(N_KV_HEADS):
        vto_ref[0, g] = jnp.concatenate([vt[g * HEAD_DIM:(g + 1) * HEAD_DIM], ones], axis=0).astype(BF16)
    kiw = p["kiw"]
    kib_ref[0] = kiw[:, :IDX_DIM].astype(BF16)
    kiwt = kiw.T
    kit32_ref[0] = kiwt[:IDX_DIM]
    wit_ref[0] = kiwt[IDX_DIM:IDX_DIM + IDX_HEADS]
    for c, chunk in enumerate(p["qic"]):
        ct = chunk.T.astype(BF16)
        for half in range(LANES // IDX_DIM):
            qit_ref[0, c * (LANES // IDX_DIM) + half] = ct[half * IDX_DIM:(half + 1) * IDX_DIM]

    @pl.when(i == 0)
    def _():
        ubuf[0:HALO, :] = jnp.zeros((HALO, ubuf.shape[1]), F32)

    @pl.when(i > 0)
    def _():
        ubuf[0:HALO, :] = ubuf[tt:tt + HALO, :]

    ubuf[HALO:HALO + tt, :] = p["u"]
    ctail_ref[0] = ubuf[tt:tt + HALO, :]

    span = ush.shape[1]
    for s in range(1, 8):
        ush[s - 1] = ubuf[pl.ds(s, span), :]

    rc_rows = 64
    cb = cb_ref[...]
    lng = lng_ref[...]
    lnb = lnb_ref[...]
    off = HALO - (CONV_WIDTH - 1)
    for r in range(tt // rc_rows):
        acc = jnp.broadcast_to(cb, (rc_rows, cb.shape[1]))
        for j in range(CONV_WIDTH):
            a, s = divmod(off + j, 8)
            rows = pl.ds(8 * a + r * rc_rows, rc_rows)
            win = ubuf[rows, :] if s == 0 else ush[s - 1, rows, :]
            acc = acc + win * cw_ref[j:j + 1, :]
        co_ref[0, r * rc_rows:(r + 1) * rc_rows, :] = _conv_post(acc, lng, lnb).astype(BF16)


def _inproj_sample_kernel(x_ref, g_ref, wm_ref, wih_ref, wil_ref, rc_ref, ra_ref, rb_ref,
                          cw_ref, cb_ref, lng_ref, lnb_ref, st_ref,
                          qpad_ref, k32_ref, v32_ref, kb_ref, vb_ref, ki32_ref, kih_ref, kil_ref,
                          qih_ref, qil_ref, wi_ref, co_ref, cnew_ref, fbuf, uscr):
    hist, db, cch = st_ref.shape
    ds = x_ref.shape[0] // db
    p = _inproj(x_ref[...], g_ref[...], wm_ref, wih_ref, wil_ref, rc_ref[...], ra_ref[...], rb_ref[...])
    for hh in range(ATTN_HEADS):
        qpad_ref[:, hh * LANES:(hh + 1) * LANES] = p["q"][hh].astype(BF16)
    k32_ref[...] = p["k"]
    v32_ref[...] = p["v"]
    kb_ref[...] = p["k"].astype(BF16)
    vb_ref[...] = p["v"].astype(BF16)
    ki = p["kiw"][:, :IDX_DIM]
    ki32_ref[...] = ki
    kih, kil = _split(ki)
    kih_ref[...] = kih
    kil_ref[...] = kil
    wi_ref[...] = p["kiw"][:, IDX_DIM:IDX_DIM + IDX_HEADS]
    per_chunk = LANES // IDX_DIM
    for hh in range(IDX_HEADS):
        half = hh % per_chunk
        hi, lo = _split(p["qic"][hh // per_chunk][:, half * IDX_DIM:(half + 1) * IDX_DIM])
        qih_ref[hh] = hi
        qil_ref[hh] = lo

    nch = cch // LANES
    u = p["u"]
    for cc in range(nch):
        uscr[cc] = u[:, cc * LANES:(cc + 1) * LANES]
    fbuf[0:hist] = st_ref[...]
    for t in range(ds):
        for cc in range(nch):
            fbuf[hist + t, :, cc * LANES:(cc + 1) * LANES] = uscr[cc, pl.ds(t, db, stride=ds), :]
    acc = jnp.broadcast_to(cb_ref[...].reshape(1, 1, cch), (ds, db, cch))
    for j in range(CONV_WIDTH):
        acc = acc + fbuf[j:j + ds] * cw_ref[j:j + 1, :].reshape(1, 1, cch)
    y = _conv_post(acc.reshape(ds * db, cch), lng_ref[...], lnb_ref[...])
    for t in range(ds):
        for cc in range(nch):
            uscr[cc, pl.ds(t, db, stride=ds), :] = y[t * db:(t + 1) * db, cc * LANES:(cc + 1) * LANES]
    for cc in range(nch):
        co_ref[:, cc * LANES:(cc + 1) * LANES] = uscr[cc].astype(BF16)
    cnew_ref[...] = fbuf[ds:ds + hist]


def _inproj_prompt(x, g_pre, wm, wih, wil, rc, ra, rb, cw, cb, lng, lnb):
    b, t, d = x.shape
    tt = TOK_TILE
    cch = cw.shape[1]
    full = lambda a: pl.BlockSpec(a.shape, lambda bi, i: (0,) * a.ndim)
    tok = lambda w: pl.BlockSpec((1, tt, w), lambda bi, i: (bi, i, 0))
    feat = lambda w: pl.BlockSpec((1, w, tt), lambda bi, i: (bi, 0, i))
    tab = pl.BlockSpec((tt, LANES), lambda bi, i: (i, 0))
    hfeat = lambda n, w: pl.BlockSpec((1, n, w, tt), lambda bi, i: (bi, 0, 0, i))
    sds = jax.ShapeDtypeStruct
    out_shape = [sds((b, ATTN_HEADS, LANES, t), BF16), sds((b, LANES, t), F32), sds((b, LANES, t), F32),
                 sds((b, t, LANES), BF16), sds((b, N_KV_HEADS, LANES, t), BF16),
                 sds((b, IDX_DIM, t), F32), sds((b, t, IDX_DIM), BF16),
                 sds((b, IDX_HEADS, IDX_DIM, t), BF16),
                 sds((b, IDX_HEADS, t), F32), sds((b, t, cch), BF16), sds((b, HALO, cch), F32)]
    out_specs = [hfeat(ATTN_HEADS, LANES), feat(LANES), feat(LANES), tok(LANES), hfeat(N_KV_HEADS, LANES),
                 feat(IDX_DIM), tok(IDX_DIM), hfeat(IDX_HEADS, IDX_DIM), feat(IDX_HEADS), tok(cch),
                 pl.BlockSpec((1, HALO, cch), lambda bi, i: (bi, 0, 0))]
    return pl.pallas_call(
        _inproj_prompt_kernel,
        grid=(b, t // tt),
        in_specs=[tok(d), full(g_pre), full(wm), full(wih), full(wil), tab, tab, tab,
                  full(cw), full(cb), full(lng), full(lnb)],
        out_specs=out_specs,
        out_shape=out_shape,
        scratch_shapes=[pltpu.VMEM((tt + HALO, cch), F32), pltpu.VMEM((7, tt + HALO - 8, cch), F32)],
        compiler_params=pltpu.CompilerParams(dimension_semantics=("arbitrary", "arbitrary"),
                                             vmem_limit_bytes=VMEM_LIMIT),
        name="inproj_prompt",
    )(x, g_pre, wm, wih, wil, rc, ra, rb, cw, cb, lng, lnb)


def _inproj_sample(x, g_pre, wm, wih, wil, rc, ra, rb, cw, cb, lng, lnb, state):
    n, d = x.shape
    hist, db, cch = state.shape
    ds = n // db
    sds = jax.ShapeDtypeStruct
    out_shape = [sds((n, ATTN_HEADS * LANES), BF16), sds((n, LANES), F32), sds((n, LANES), F32),
                 sds((n, LANES), BF16), sds((n, LANES), BF16),
                 sds((n, IDX_DIM), F32), sds((n, IDX_DIM), BF16), sds((n, IDX_DIM), BF16),
                 sds((IDX_HEADS, n, IDX_DIM), BF16), sds((IDX_HEADS, n, IDX_DIM), BF16),
                 sds((n, IDX_HEADS), F32), sds((n, cch), BF16), sds((hist, db, cch), F32)]
    return pl.pallas_call(
        _inproj_sample_kernel,
        out_shape=out_shape,
        scratch_shapes=[pltpu.VMEM((hist + ds, db, cch), F32), pltpu.VMEM((cch // LANES, n, LANES), F32)],
        compiler_params=pltpu.CompilerParams(vmem_limit_bytes=VMEM_LIMIT),
        name="inproj_sample",
    )(x, g_pre, wm, wih, wil, rc, ra, rb, cw, cb, lng, lnb, state)


def _key_to_f32(key):
    bits = jnp.where(key < 0, key ^ INT_MAX, key)
    return lax.bitcast_convert_type(bits, F32)


def _row_reduce(sources, rows, fn, op, init):
    c = jnp.full((rows, LANES), init, F32)
    for get, nblk, width, col0 in sources:
        def body(j, c, get=get, width=width, col0=col0):
            blk = get(j)
            col = col0 + j * width + lax.broadcasted_iota(I32, blk.shape, 1)
            val = fn(blk, col)
            for cc in range(width // LANES):
                c = op(c, val[:, cc * LANES:(cc + 1) * LANES])
            return c
        c = lax.fori_loop(0, nblk, body, c)
    return c


def _fold_rows(x, op, part):
    groups = [x[r:r + part] for r in range(0, x.shape[0], part)]
    parts = groups[:FOLD_CHAINS]
    for k, g in enumerate(groups[FOLD_CHAINS:]):
        parts[k % FOLD_CHAINS] = op(parts[k % FOLD_CHAINS], g)
    while len(parts) > 1:
        parts = [op(parts[k], parts[k + 1]) if k + 1 < len(parts) else parts[k] for k in range(0, len(parts), 2)]
    return parts[0]


class _KeyMajor(NamedTuple):
    get: Callable
    nblk: Any
    q: int


def _vshape(sources, rows):
    return (1, sources.q) if isinstance(sources, _KeyMajor) else (rows, 1)


def _scan(sources, rows, fn, op, red, init):
    if not isinstance(sources, _KeyMajor):
        return red(_row_reduce(sources, rows, fn, op, init), axis=1, keepdims=True)
    def body(j, c):
        for blk, key0 in sources.get(j):
            key = key0 + lax.broadcasted_iota(I32, blk.shape, 0)
            c = op(c, _fold_rows(fn(blk, key), op, 8))
        return c

    c = lax.fori_loop(0, sources.nblk, body, jnp.full((8, sources.q), init, F32))
    return red(c, axis=0, keepdims=True)


def _count(sources, rows, pred):
    return _scan(sources, rows, lambda blk, col: jnp.where(pred(blk, col), 1.0, 0.0), jnp.add, jnp.sum, 0.0)


def _topk_select(sources, rows, topk, vmin, vmax, nvis, cpos, cnn):
    kf = float(topk)
    few = nvis <= kf
    zero_top = (cpos <= kf) & (cnn >= kf) & jnp.logical_not(few)
    up = cpos > kf

    def unsettled(carry):
        return (carry[0] < BISECT_STEPS) & (jnp.min(carry[6]) < 0.5)

    logit = lambda c: jnp.log((c + 0.5) / (nvis - c + 0.5))
    z_k = logit(kf)

    def step(carry):
        it, lo, hi, clo, chi, thr, done = carry
        z_lo, z_hi = logit(clo), logit(chi)
        den = z_lo - z_hi
        t = jnp.clip(jnp.where(den > 0.0, (z_lo - z_k) / den, 0.5), 1.0 / 32.0, 31.0 / 32.0)
        t = jnp.where((it & 3) == 3, 0.5, t)
        mid = lo + (hi - lo) * t
        c = _count(sources, rows, lambda blk, col: blk > mid)
        hit = (c == kf) & (done < 0.5)
        above, below = c > kf, c < kf
        return (it + 1, jnp.where(above, mid, lo), jnp.where(below, mid, hi),
                jnp.where(above, c, clo), jnp.where(below, c, chi),
                jnp.where(hit, mid, thr), jnp.where(hit, 1.0, done))

    carry0 = (jnp.int32(0), jnp.where(up, 0.0, vmin), jnp.where(up, vmax, 0.0),
              jnp.where(up, cpos, nvis), jnp.where(up, 0.0, cpos),
              jnp.where(few, -jnp.inf, 0.0), jnp.where(few | zero_top, 1.0, 0.0))
    thr, done = lax.while_loop(unsettled, step, carry0)[5:]
    settled = done > 0.5
    zero_need = kf - cpos
    cidx = jnp.where(zero_top & (zero_need > 0.0), INT_MAX, -1)
    need = jnp.where(zero_top, zero_need, 0.0)
    cut = jnp.where(zero_top & (cnn > kf) & (zero_need > 0.0), 1.0, 0.0)

    def exact(_):
        thr_e, cnt_e = _kth_largest_exact(sources, rows, topk)
        need_e = kf - _count(sources, rows, lambda blk, col: blk > thr_e)
        fin = thr_e > -jnp.inf
        return (jnp.where(settled, thr, thr_e), jnp.where(settled, cidx, jnp.where(fin, INT_MAX, -1)),
                jnp.where(settled, need, need_e), jnp.where(settled, cut, jnp.where(fin & (cnt_e > kf), 1.0, 0.0)))

    return lax.cond(jnp.min(done) < 0.5, exact, lambda _: (thr, cidx, need, cut), 0)


def _tie_cutoffs(slab_sources, nrows, ncols_max, thr_ref, need_ref, cut_ref, cidx_ref):
    n_it = int(np.ceil(np.log2(ncols_max + 1))) + 1

    def slab(sl, _):
        rs = pl.ds(pl.multiple_of(sl * 8, 8), 8)
        cut = cut_ref[rs, :] > 0.5

        @pl.when(jnp.max(cut_ref[rs, :]) > 0.5)
        def _():
            thr = thr_ref[rs, :]
            need = need_ref[rs, :]
            src = slab_sources(pl.multiple_of(sl * 8, 8))

            def bis_c(_, carry):
                lo_c, hi_c = carry
                mid = (lo_c + hi_c) >> 1
                ge = _count(src, 8, lambda blk, col: (blk == thr) & (col <= mid)) >= need
                return jnp.where(ge, lo_c, mid), jnp.where(ge, mid, hi_c)

            _, hi_c = lax.fori_loop(0, n_it, bis_c, (jnp.full((8, 1), -1, I32),
                                                     jnp.full((8, 1), ncols_max - 1, I32)))
            cidx_ref[rs, :] = jnp.where(cut, hi_c, cidx_ref[rs, :])
        return 0

    lax.fori_loop(0, nrows // 8, slab, 0)


def _drop_excess_ties(get_blk, put_blk, nblk, width, q, thr, need, cut):
    tri = jnp.where(lax.broadcasted_iota(I32, (width, width), 1) <= lax.broadcasted_iota(I32, (width, width), 0),
                    1.0, 0.0).astype(BF16)
    marked = cut > 0.5

    def body(j, before):
        blk = get_blk(j)
        tied = (blk == thr) & marked
        rank = _dot(tri, jnp.where(tied, 1.0, 0.0).astype(BF16))
        put_blk(j, jnp.where(tied & (before + rank > need), -jnp.inf, blk))
        return before + jnp.max(rank, axis=0, keepdims=True)

    lax.fori_loop(0, nblk, body, jnp.zeros((1, q), F32))


def _kth_largest_exact(sources, rows, topk):
    kf = float(topk)

    def bis(_, carry):
        lo, hi = carry
        mid = (lo >> 1) + (hi >> 1) + (lo & hi & 1)
        thr = _key_to_f32(mid)
        ge = _count(sources, rows, lambda blk, col: blk >= thr) >= kf
        return jnp.where(ge, mid, lo), jnp.where(ge, hi, mid)

    lo0 = jnp.full(_vshape(sources, rows), KEY_NEG_INF, I32)
    hi0 = jnp.full(_vshape(sources, rows), KEY_NAN_HI, I32)
    _, hi = lax.fori_loop(0, 32, bis, (lo0, hi0))

    def below(upper):
        m = _scan(sources, rows, lambda blk, col: jnp.where(blk < upper, blk, -jnp.inf),
                  jnp.maximum, jnp.max, -jnp.inf)
        return m, _count(sources, rows, lambda blk, col: blk >= m)

    def short(carry):
        return jnp.max(jnp.where(carry[2] < kf, 1.0, 0.0)) > 0.0

    def lower(carry):
        upper, m, c = carry
        upper = jnp.where(c < kf, m, upper)
        m, c = below(upper)
        return upper, m, c

    upper0 = _key_to_f32(hi)
    m0, c0 = below(upper0)
    _, thr, cnt = lax.while_loop(short, lower, (upper0, m0, c0))
    return thr, cnt


def _selected(blk, col, thr, cidx):
    return (blk > thr) | ((blk == thr) & (col <= cidx))


def _attn_prompt_kernel(qt_ref, qit_ref, wit_ref, kib_ref, kb_ref, vto_ref, o_ref,
                        sc_ref, vmx_ref, vmn_ref, cp_ref, cn_ref, thr_ref, cidx_ref, m_ref, acc_ref, *, topk):
    i = pl.program_id(1)
    tq = qt_ref.shape[3]
    tk = sc_ref.shape[1]
    nkb = (i * tq + tq + tk - 1) // tk
    q_pos = i * tq + lax.broadcasted_iota(I32, (tk, tq), 1)
    key_of = lambda j: j * tk + lax.broadcasted_iota(I32, (tk, tq), 0)
    wit = wit_ref[0]

    vmx_ref[...] = jnp.full(vmx_ref.shape, -jnp.inf, F32)
    vmn_ref[...] = jnp.full(vmn_ref.shape, jnp.inf, F32)
    cp_ref[...] = jnp.zeros(cp_ref.shape, F32)
    cn_ref[...] = jnp.zeros(cn_ref.shape, F32)

    def score_block(j, _):
        kib = kib_ref[0, pl.ds(pl.multiple_of(j * tk, tk), tk), :]
        acc = jnp.zeros((tk, tq), F32)
        for hh in range(IDX_HEADS):
            acc = acc + jnp.maximum(_dot(kib, qit_ref[0, hh]), 0.0) * wit[hh:hh + 1, :]
        vis = key_of(j) <= q_pos
        sc = jnp.where(vis, acc, -jnp.inf)
        sc_ref[j] = sc
        vmx_ref[...] = jnp.maximum(vmx_ref[...], _fold_rows(sc, jnp.maximum, 8))
        vmn_ref[...] = jnp.minimum(vmn_ref[...], _fold_rows(jnp.where(vis, acc, jnp.inf), jnp.minimum, 8))
        cp_ref[...] = cp_ref[...] + _fold_rows(jnp.where(sc > 0.0, 1.0, 0.0), jnp.add, 8)
        cn_ref[...] = cn_ref[...] + _fold_rows(jnp.where(sc >= 0.0, 1.0, 0.0), jnp.add, 8)
        return 0

    lax.fori_loop(0, nkb, score_block, 0)

    @pl.when(nkb < sc_ref.shape[0])
    def _():
        sc_ref[nkb] = jnp.full((tk, tq), -jnp.inf, F32)

    pairs = _KeyMajor(lambda j: [(sc_ref[2 * j], 2 * j * tk), (sc_ref[2 * j + 1], (2 * j + 1) * tk)],
                      (nkb + 1) // 2, tq)
    nvis = (i * tq + 1 + lax.broadcasted_iota(I32, (1, tq), 1)).astype(F32)
    thr, cidx, need, cut = _topk_select(
        pairs, tq, topk, jnp.min(vmn_ref[...], axis=0, keepdims=True), jnp.max(vmx_ref[...], axis=0, keepdims=True),
        nvis, jnp.sum(cp_ref[...], axis=0, keepdims=True), jnp.sum(cn_ref[...], axis=0, keepdims=True))
    thr_ref[...] = thr
    cidx_ref[...] = cidx

    @pl.when(jnp.max(cut) > 0.5)
    def _():
        def put(j, val):
            sc_ref[j] = val
        _drop_excess_ties(lambda j: sc_ref[j], put, nkb, tk, tq, thr, need, cut)

    m_ref[...] = jnp.full(m_ref.shape, NEG, F32)
    acc_ref[...] = jnp.zeros(acc_ref.shape, F32)
    thr = thr_ref[...]
    cidx = cidx_ref[...]

    def attn_block(j, _):
        ks = pl.ds(pl.multiple_of(j * tk, tk), tk)
        bias = jnp.where(_selected(sc_ref[j], key_of(j), thr, cidx), 0.0, NEG).astype(BF16)
        kb = kb_ref[0, ks, :]
        qk = lambda hh: _dot(kb, qt_ref[0, hh])
        s_next = qk(0)
        pending = None
        for hh in range(ATTN_HEADS + 1):
            if hh < ATTN_HEADS:
                s = s_next.astype(BF16) + bias
                if hh + 1 < ATTN_HEADS:
                    s_next = qk(hh + 1)
                blk_max = jnp.max(_fold_rows(s, jnp.maximum, 16).astype(F32), axis=0, keepdims=True)
                m_old = m_ref[hh]
                m_new = jnp.maximum(m_old, blk_max)
                alpha = jnp.exp2(m_old - m_new)
                p = jnp.exp2(s - m_new.astype(BF16))
                m_ref[hh] = m_new
            if pending is not None:
                ph, palpha, pp = pending
                acc_ref[ph] = palpha * acc_ref[ph] + _dot(vto_ref[0, ph // KV_GROUP, :, ks], pp)
            pending = (hh, alpha, p) if hh < ATTN_HEADS else None
        return 0

    lax.fori_loop(0, nkb, attn_block, 0)

    outs = []
    for hh in range(ATTN_HEADS):
        acc = acc_ref[hh]
        outs.append((acc[:HEAD_DIM] / acc[HEAD_DIM:]).T)
    o_ref[0] = jnp.concatenate(outs, axis=1).astype(BF16)


def _attn_prompt(qt, qit, wit, kib, kb, vto, topk):
    b, _, _, t = qt.shape
    tq = Q_TILE
    tk = K_BLOCK
    assert t % tq == 0 and t % tk == 0 and tk % tq == 0
    hfeat = lambda n, w: pl.BlockSpec((1, n, w, tq), lambda bi, i: (bi, 0, 0, i))
    seq = lambda w: pl.BlockSpec((1, t, w), lambda bi, i: (bi, 0, 0))
    return pl.pallas_call(
        functools.partial(_attn_prompt_kernel, topk=topk),
        grid=(b, t // tq),
        in_specs=[hfeat(ATTN_HEADS, LANES), hfeat(IDX_HEADS, IDX_DIM),
                  pl.BlockSpec((1, IDX_HEADS, tq), lambda bi, i: (bi, 0, i)), seq(IDX_DIM), seq(LANES),
                  pl.BlockSpec((1, N_KV_HEADS, LANES, t), lambda bi, i: (bi, 0, 0, 0))],
        out_specs=pl.BlockSpec((1, tq, ATTN_WIDTH), lambda bi, i: (bi, i, 0)),
        out_shape=jax.ShapeDtypeStruct((b, t, ATTN_WIDTH), BF16),
        scratch_shapes=[pltpu.VMEM((t // tk + (t // tk) % 2, tk, tq), F32)] + [pltpu.VMEM((8, tq), F32)] * 4
                       + [pltpu.VMEM((1, tq), F32), pltpu.VMEM((1, tq), I32),
                          pltpu.VMEM((ATTN_HEADS, 1, tq), F32), pltpu.VMEM((ATTN_HEADS, LANES, tq), F32)],
        compiler_params=pltpu.CompilerParams(dimension_semantics=("arbitrary", "arbitrary"),
                                             vmem_limit_bytes=VMEM_LIMIT),
        name="attn_prompt",
    )(qt, qit, wit, kib, kb, vto)


def _score_sample_kernel(pt_ref, qih_ref, qil_ref, wcol_ref, knh_ref, knl_ref, *rest):
    pages = rest[:PAGES_PER_STEP]
    scp_ref, scn_ref, kh_s, kl_s = rest[PAGES_PER_STEP:]
    c = pl.program_id(1)
    ds = qih_ref.shape[1]
    page = pages[0].shape[1]
    qh = qih_ref[...].reshape(IDX_HEADS * ds, IDX_DIM)
    ql = qil_ref[...].reshape(IDX_HEADS * ds, IDX_DIM)
    wcol = wcol_ref[0]

    def combine(s):
        r = jnp.maximum(s, 0.0) * wcol
        return jnp.sum(r.reshape(IDX_HEADS, ds, r.shape[1]), axis=0)

    for pg in range(PAGES_PER_STEP):
        kh, kl = _split(pages[pg][...])
        kh_s[:, pg * page:(pg + 1) * page] = kh
        kl_s[:, pg * page:(pg + 1) * page] = kl
    kh = kh_s[...]
    scp_ref[0] = combine(_dot(qh, kh) + _dot(ql, kh) + _dot(qh, kl_s[...]))

    @pl.when(c == pl.num_programs(1) - 1)
    def _():
        knh = knh_ref[0]
        sn = combine(_dot_nt(qh, knh) + _dot_nt(ql, knh) + _dot_nt(qh, knl_ref[0]))
        tok = lax.broadcasted_iota(I32, sn.shape, 0)
        col = lax.broadcasted_iota(I32, sn.shape, 1)
        scn_ref[0] = jnp.where(col <= tok, sn, -jnp.inf)


def _select_sample_kernel(scp_ref, scn_ref, thr_ref, cidx_ref, need_ref, cut_ref, *, topk, blk_w):
    rows, past = scp_ref.shape

    def sources(rs):
        return [(lambda j: scp_ref[rs, pl.ds(pl.multiple_of(j * blk_w, blk_w), blk_w)], past // blk_w, blk_w, 0),
                (lambda j: scn_ref[rs, :], 1, scn_ref.shape[1], past)]

    src = sources(slice(None))
    lane_red = lambda fn, op, init, red: red(_row_reduce(src, rows, fn, op, init), axis=1, keepdims=True)
    vmax = lane_red(lambda blk, col: blk, jnp.maximum, -jnp.inf, jnp.max)
    vmin = lane_red(lambda blk, col: jnp.where(blk == -jnp.inf, jnp.inf, blk), jnp.minimum, jnp.inf, jnp.min)
    nvis = _count(src, rows, lambda blk, col: blk > -jnp.inf)
    cpos = _count(src, rows, lambda blk, col: blk > 0.0)
    cnn = _count(src, rows, lambda blk, col: blk >= 0.0)
    thr, cidx, need, cut = _topk_select(src, rows, topk, vmin, vmax, nvis, cpos, cnn)
    thr_ref[...] = thr
    cidx_ref[...] = cidx
    need_ref[...] = need
    cut_ref[...] = cut
    _tie_cutoffs(lambda r0: sources(pl.ds(r0, 8)), rows, past + scn_ref.shape[1],
                 thr_ref, need_ref, cut_ref, cidx_ref)


def _attn_sample_kernel(pt_ref, qpad_ref, scp_ref, scn_ref, thr_ref, cidx_ref, knew_ref, vnew_ref, *rest,
                        past_len):
    kpages = rest[:PAGES_PER_STEP]
    vpages = rest[PAGES_PER_STEP:2 * PAGES_PER_STEP]
    o_ref, kt_s, vt_s, m_ref, l_ref, acc_ref = rest[2 * PAGES_PER_STEP:]
    c = pl.program_id(1)
    ds = qpad_ref.shape[0]
    page = kpages[0].shape[1]
    thr = thr_ref[...]
    cidx = cidx_ref[...]

    @pl.when(c == 0)
    def _():
        m_ref[...] = jnp.full(m_ref.shape, NEG, F32)
        l_ref[...] = jnp.zeros(l_ref.shape, F32)
        acc_ref[...] = jnp.zeros(acc_ref.shape, F32)

    q_all = jnp.concatenate([qpad_ref[:, hh * LANES:(hh + 1) * LANES] for hh in range(ATTN_HEADS)], axis=0)

    def attend(sc, col, s_of_q, pv):
        bias1 = jnp.where(_selected(sc, col, thr, cidx), 0.0, NEG)
        s = s_of_q + jnp.concatenate([bias1] * ATTN_HEADS, axis=0)
        m_old = m_ref[...]
        m_new = jnp.maximum(m_old, jnp.max(s, axis=1, keepdims=True))
        alpha = jnp.exp2(m_old - m_new)
        p = jnp.exp2(s - m_new)
        l_ref[...] = alpha * l_ref[...] + jnp.sum(p, axis=1, keepdims=True)
        acc_ref[...] = alpha * acc_ref[...] + pv(p.astype(BF16))
        m_ref[...] = m_new

    for pg in range(PAGES_PER_STEP):
        kt_s[:, pg * page:(pg + 1) * page] = kpages[pg][...].astype(BF16)
        vt_s[:, pg * page:(pg + 1) * page] = vpages[pg][...].astype(BF16)
    width = PAGES_PER_STEP * page
    col = c * width + lax.broadcasted_iota(I32, (ds, width), 1)
    attend(scp_ref[0], col, _dot(q_all, kt_s[...]), lambda p: _dot_nt(p, vt_s[...]))

    @pl.when(c == pl.num_programs(1) - 1)
    def _():
        sc = scn_ref[0]
        ncol = past_len + lax.broadcasted_iota(I32, sc.shape, 1)
        attend(sc, ncol, _dot_nt(q_all, knew_ref[0]), lambda p: _dot(p, vnew_ref[0]))
        o = acc_ref[...] / l_ref[...]
        outs = []
        for hh in range(ATTN_HEADS):
            g = hh // KV_GROUP
            outs.append(o[hh * ds:(hh + 1) * ds, g * HEAD_DIM:(g + 1) * HEAD_DIM])
        o_ref[...] = jnp.concatenate(outs, axis=1).astype(BF16)


def _attn_sample(qpad, qih, qil, wi, kih, kil, kb, vb, pool_kt, pool_vt, pool_ikt, page_table, topk, db):
    n = qpad.shape[0]
    ds = n // db
    n_pool, _, page = pool_ikt.shape
    n_pages = page_table.shape[1]
    past_len = n_pages * page
    pps = PAGES_PER_STEP
    n_chunks = n_pages // pps
    assert n_pages % pps == 0 and ds <= LANES

    wcol = wi.reshape(db, ds, IDX_HEADS).transpose(0, 2, 1).reshape(db, IDX_HEADS * ds, 1)
    padk = lambda a: jnp.pad(a.reshape(db, ds, a.shape[1]), ((0, 0), (0, LANES - ds), (0, 0)))
    knh, knl, knew, vnew = padk(kih), padk(kil), padk(kb), padk(vb)

    def page_spec(w, pg):
        return pl.BlockSpec((None, w, page), lambda b, c, pt: (pt[b, c * pps + pg], 0, 0))

    hm = pl.BlockSpec((IDX_HEADS, ds, IDX_DIM), lambda b, c, pt: (0, b, 0))
    seq3 = lambda r, w: pl.BlockSpec((1, r, w), lambda b, c, pt: (b, 0, 0))
    chunk = pl.BlockSpec((1, ds, pps * page), lambda b, c, pt: (b, 0, c))
    scp, scn = pl.pallas_call(
        _score_sample_kernel,
        grid_spec=pltpu.PrefetchScalarGridSpec(
            num_scalar_prefetch=1, grid=(db, n_chunks),
            in_specs=[hm, hm, seq3(IDX_HEADS * ds, 1), seq3(LANES, IDX_DIM), seq3(LANES, IDX_DIM)]
                     + [page_spec(IDX_DIM, pg) for pg in range(pps)],
            out_specs=[chunk, seq3(ds, LANES)],
            scratch_shapes=[pltpu.VMEM((IDX_DIM, pps * page), BF16), pltpu.VMEM((IDX_DIM, pps * page), BF16)]),
        out_shape=[jax.ShapeDtypeStruct((db, ds, past_len), F32), jax.ShapeDtypeStruct((db, ds, LANES), F32)],
        compiler_params=pltpu.CompilerParams(dimension_semantics=("arbitrary", "arbitrary"),
                                             vmem_limit_bytes=VMEM_LIMIT),
        name="score_sample",
    )(page_table, qih, qil, wcol, knh, knl, *([pool_ikt] * pps))

    rows = min(SEL_ROWS, n)
    blk_w = 512
    thr, cidx = pl.pallas_call(
        functools.partial(_select_sample_kernel, topk=topk, blk_w=blk_w),
        grid=(n // rows,),
        in_specs=[pl.BlockSpec((rows, past_len), lambda r: (r, 0)), pl.BlockSpec((rows, LANES), lambda r: (r, 0))],
        out_specs=[pl.BlockSpec((rows, 1), lambda r: (r, 0)), pl.BlockSpec((rows, 1), lambda r: (r, 0))],
        out_shape=[jax.ShapeDtypeStruct((n, 1), F32), jax.ShapeDtypeStruct((n, 1), I32)],
        scratch_shapes=[pltpu.VMEM((rows, 1), F32), pltpu.VMEM((rows, 1), F32)],
        compiler_params=pltpu.CompilerParams(dimension_semantics=("arbitrary",), vmem_limit_bytes=VMEM_LIMIT),
        name="select_sample",
    )(scp.reshape(n, past_len), scn.reshape(n, LANES))

    tokrows = lambda w: pl.BlockSpec((ds, w), lambda b, c, pt: (b, 0))
    nrow = ATTN_HEADS * ds
    return pl.pallas_call(
        functools.partial(_attn_sample_kernel, past_len=past_len),
        grid_spec=pltpu.PrefetchScalarGridSpec(
            num_scalar_prefetch=1, grid=(db, n_chunks),
            in_specs=[tokrows(ATTN_HEADS * LANES), chunk, seq3(ds, LANES),
                      tokrows(1), tokrows(1), seq3(LANES, LANES), seq3(LANES, LANES)]
                     + [page_spec(LANES, pg) for pg in range(pps)] * 2,
            out_specs=tokrows(ATTN_WIDTH),
            scratch_shapes=[pltpu.VMEM((LANES, pps * page), BF16), pltpu.VMEM((LANES, pps * page), BF16),
                            pltpu.VMEM((nrow, 1), F32), pltpu.VMEM((nrow, 1), F32),
                            pltpu.VMEM((nrow, LANES), F32)]),
        out_shape=jax.ShapeDtypeStruct((n, ATTN_WIDTH), BF16),
        compiler_params=pltpu.CompilerParams(dimension_semantics=("arbitrary", "arbitrary"),
                                             vmem_limit_bytes=VMEM_LIMIT),
        name="attn_sample",
    )(page_table, qpad, scp, scn, thr, cidx, knew, vnew, *([pool_kt] * pps), *([pool_vt] * pps))


def _mix_mlp_kernel(a_ref, c_ref, x_ref, woa_ref, woc_ref, g1_ref, g2_ref, wup_ref, wdn_ref, g3_ref, y_ref):
    m = _dot(a_ref[...], woa_ref[...]) + _dot(c_ref[...], woc_ref[...])
    x1 = x_ref[...] + _rms(m) * g1_ref[...]
    h = (_rms(x1) * g2_ref[...]).astype(BF16)
    dff = wup_ref.shape[1]
    f = jnp.zeros(x1.shape, F32)
    for cc in range(dff // FF_CHUNK):
        cs = slice(cc * FF_CHUNK, (cc + 1) * FF_CHUNK)
        up = jnp.maximum(_dot(h, wup_ref[:, cs]), 0.0)
        f = f + _dot((up * up).astype(BF16), wdn_ref[cs, :])
    y_ref[...] = x1 + _rms(f) * g3_ref[...]


def _mix_mlp(a, c, x, woa, woc, g1, g2, wup, wdn, g3):
    n, d = x.shape
    tm = min(MLP_TILE, n)
    full = lambda w: pl.BlockSpec(w.shape, lambda i: (0,) * w.ndim)
    tok = lambda w: pl.BlockSpec((tm, w), lambda i: (i, 0))
    return pl.pallas_call(
        _mix_mlp_kernel,
        grid=(n // tm,),
        in_specs=[tok(a.shape[1]), tok(c.shape[1]), tok(d), full(woa), full(woc), full(g1), full(g2),
                  full(wup), full(wdn), full(g3)],
        out_specs=tok(d),
        out_shape=jax.ShapeDtypeStruct((n, d), F32),
        compiler_params=pltpu.CompilerParams(dimension_semantics=("arbitrary",), vmem_limit_bytes=VMEM_LIMIT),
        name="mix_mlp",
    )(a, c, x, woa, woc, g1, g2, wup, wdn, g3)


def _rope_tables(pos):
    rot = HEAD_DIM // 4
    half = rot // 2
    inv_freq = jnp.power(ROPE_THETA, -jnp.arange(half, dtype=F32) * (2.0 / rot))
    ang = pos.astype(F32)[:, None] * inv_freq[None, :]
    cos, sin = jnp.cos(ang), jnp.sin(ang)
    n = pos.shape[0]
    zeros = lambda w: jnp.zeros((n, w), F32)
    reps = LANES // HEAD_DIM
    rc = jnp.concatenate([cos, cos, jnp.ones((n, HEAD_DIM - rot), F32)] * reps, axis=1)
    ra = jnp.concatenate([zeros(half), sin, zeros(HEAD_DIM - rot)] * reps, axis=1)
    rb = jnp.concatenate([-sin, zeros(HEAD_DIM - half)] * reps, axis=1)
    return rc, ra, rb


def _prep_w_in(w):
    d = w.shape[0]
    sizes = (ATTN_WIDTH, N_KV_HEADS * HEAD_DIM, N_KV_HEADS * HEAD_DIM, IDX_HEADS * IDX_DIM, IDX_DIM, IDX_HEADS)
    offs = np.cumsum((0,) + sizes)
    wq, wk, wv, wqi, wki, wwi = [w[:, offs[i]:offs[i + 1]] for i in range(6)]
    wu = w[:, offs[6]:]
    zpad = jnp.zeros((d, HEAD_DIM), w.dtype)
    qcols = []
    for hh in range(ATTN_HEADS):
        piece = wq[:, hh * HEAD_DIM:(hh + 1) * HEAD_DIM]
        qcols += [piece, zpad] if hh // KV_GROUP == 0 else [zpad, piece]
    wm = jnp.concatenate(qcols + [wk, wv, wu], axis=1).astype(BF16)
    widx = jnp.concatenate([wqi, wki, wwi, jnp.zeros((d, LANES - IDX_DIM - IDX_HEADS), w.dtype)], axis=1)
    wih, wil = _split(widx)
    return wm, wih, wil


def kernel(x_prompt, x_sample, cache_k, cache_v, cache_idx_k, state_conv, page_table, norm_mix_pre, w_in,
           conv_dw_w, conv_dw_b, conv_ln_g, conv_ln_b, w_out, norm_mix_post, norm_mlp_pre, w_up, w_down,
           norm_mlp_post):
    depth = w_in.shape[0]
    b, t, d = x_prompt.shape
    db, ds, _ = x_sample.shape
    n_pool, page = cache_k.shape[1], cache_k.shape[2]
    past_len = page_table.shape[1] * page
    topk_p = min(INDEX_TOPK, t // 4)
    topk_s = min(INDEX_TOPK, (past_len + ds) // 4)
    hist = CONV_WIDTH - 1

    rcp, rap, rbp = _rope_tables(jnp.arange(t, dtype=I32))
    rcs, ras, rbs = [jnp.tile(a, (db, 1)) for a in _rope_tables(past_len + jnp.arange(ds, dtype=I32))]

    xp = x_prompt
    xs = x_sample.reshape(db * ds, d)
    outs = [[] for _ in range(8)]
    row = lambda a: a.reshape(1, -1)
    for l in range(depth):
        wm, wih, wil = _prep_w_in(w_in[l])
        conv_args = (conv_dw_w[l], row(conv_dw_b[l]), row(conv_ln_g[l]), row(conv_ln_b[l]))
        woa = w_out[l, :ATTN_WIDTH].astype(BF16)
        woc = w_out[l, ATTN_WIDTH:].astype(BF16)
        mlp_args = (woa, woc, row(norm_mix_post[l]), row(norm_mlp_pre[l]), w_up[l].astype(BF16),
                    w_down[l].astype(BF16), row(norm_mlp_post[l]))

        (qt, kt32, vt32, kb, vto, kit32, kib, qit, wit, co, ctail) = _inproj_prompt(
            xp, row(norm_mix_pre[l]), wm, wih, wil, rcp, rap, rbp, *conv_args)
        ao = _attn_prompt(qt, qit, wit, kib, kb, vto, topk_p)
        xp = _mix_mlp(ao.reshape(b * t, -1), co.reshape(b * t, -1), xp.reshape(b * t, d), *mlp_args).reshape(b, t, d)
        unt = lambda a: a.reshape(b, N_KV_HEADS, HEAD_DIM, t).transpose(0, 3, 1, 2)
        outs[0].append(unt(kt32))
        outs[1].append(unt(vt32))
        outs[2].append(kit32.transpose(0, 2, 1))
        outs[3].append(ctail[:, HALO - hist:])

        pool_kt = cache_k[l].transpose(0, 2, 3, 1).reshape(n_pool, N_KV_HEADS * HEAD_DIM, page)
        pool_vt = cache_v[l].transpose(0, 2, 3, 1).reshape(n_pool, N_KV_HEADS * HEAD_DIM, page)
        pool_ikt = cache_idx_k[l].transpose(0, 2, 1)
        (qpad, k32, v32, kb, vb, ki32, kih, kil, qih, qil, wi, co, cnew) = _inproj_sample(
            xs, row(norm_mix_pre[l]), wm, wih, wil, rcs, ras, rbs, *conv_args, state_conv[l].transpose(1, 0, 2))
        ao = _attn_sample(qpad, qih, qil, wi, kih, kil, kb, vb, pool_kt, pool_vt, pool_ikt,
                          page_table, topk_s, db)
        xs = _mix_mlp(ao, co, xs, *mlp_args)
        outs[4].append(k32.reshape(db, ds, N_KV_HEADS, HEAD_DIM))
        outs[5].append(v32.reshape(db, ds, N_KV_HEADS, HEAD_DIM))
        outs[6].append(ki32.reshape(db, ds, IDX_DIM))
        outs[7].append(cnew.transpose(1, 0, 2))

    st = [jnp.stack(o, 0) for o in outs]
    return (xp, xs.reshape(db, ds, d), st[0], st[1], st[2], st[3], st[4], st[5], st[6], st[7])
```

```python
import functools
from typing import Any, Callable, NamedTuple

import numpy as np
import jax
import jax.numpy as jnp
from jax import lax
from jax.experimental import pallas as pl
from jax.experimental.pallas import tpu as pltpu

F32 = jnp.float32
BF16 = jnp.bfloat16
I32 = jnp.int32

ATTN_HEADS = 8
HEAD_DIM = 64
ATTN_WIDTH = ATTN_HEADS * HEAD_DIM
N_KV_HEADS = 2
KV_GROUP = ATTN_HEADS // N_KV_HEADS
IDX_HEADS = 8
IDX_DIM = 64
INDEX_TOPK = 256
ROPE_THETA = 500000.0
CONV_WIDTH = 31
RMS_EPS = 1e-6
LN_EPS = 1e-5
SCALE = 0.125
assert HEAD_DIM ** -0.5 == SCALE and IDX_DIM ** -0.5 == SCALE
assert N_KV_HEADS * HEAD_DIM == 128

LANES = 128
LOG2E = 1.4426950408889634
NEG = -(2.0 ** 100)
HALO = 32
TOK_TILE = 256
Q_TILE = 256
K_BLOCK = 512
SEL_ROWS = 128
FOLD_CHAINS = 4
BISECT_STEPS = 28
MLP_TILE = 256
FF_CHUNK = 1024
PAGES_PER_STEP = 32
VMEM_LIMIT = 56 * 1024 * 1024

KEY_NEG_INF = -2139095041
KEY_NAN_HI = 2139095041
INT_MAX = 2147483647


def _dot(a, b):
    return jnp.dot(a, b, preferred_element_type=F32)


def _dot_nt(a, b):
    return lax.dot_general(a, b, (((1,), (1,)), ((), ())), preferred_element_type=F32)


def _split(x):
    hi = x.astype(BF16)
    lo = (x - hi.astype(F32)).astype(BF16)
    return hi, lo


def _rms(x):
    return x * lax.rsqrt(jnp.mean(x * x, axis=-1, keepdims=True) + RMS_EPS)


def _sigmoid(x):
    return 1.0 / (1.0 + jnp.exp(-x))


def _inproj(x, g_pre, wm_ref, wih_ref, wil_ref, rc, ra, rb, on_glu):
    h = _rms(x) * g_pre
    h_hi, h_lo = _split(h)
    nq = ATTN_HEADS * LANES
    nqkv = nq + 2 * LANES
    cch = wm_ref.shape[1] - nqkv
    zu = _dot(h_hi, wm_ref[:, nqkv:])
    u = zu[:, :cch // 2] * _sigmoid(zu[:, cch // 2:])
    on_glu(u)
    zm = _dot(h_hi, wm_ref[:, :nqkv])
    wih = wih_ref[...]
    zi = _dot(h_hi, wih) + _dot(h_lo, wih) + _dot(h_hi, wil_ref[...])

    def rope(c):
        return c * rc + pltpu.roll(c, 8, 1) * ra + pltpu.roll(c, LANES - 8, 1) * rb

    q = [rope(zm[:, c * LANES:(c + 1) * LANES]) * (SCALE * LOG2E) for c in range(ATTN_HEADS)]
    k = rope(zm[:, nq:nq + LANES])
    v = zm[:, nq + LANES:nq + 2 * LANES]
    qic = [rope(zi[:, c * LANES:(c + 1) * LANES]) * SCALE for c in range(IDX_HEADS * IDX_DIM // LANES)]
    kc = zi[:, IDX_HEADS * IDX_DIM:IDX_HEADS * IDX_DIM + LANES]
    kiw = jnp.where(lax.broadcasted_iota(I32, kc.shape, 1) < IDX_DIM, rope(kc), kc)
    return dict(q=q, k=k, v=v, u=u, qic=qic, kiw=kiw)


def _conv_post(acc, lng, lnb):
    mu = jnp.mean(acc, axis=-1, keepdims=True)
    d = acc - mu
    var = jnp.mean(d * d, axis=-1, keepdims=True)
    y = d * lax.rsqrt(var + LN_EPS) * lng + lnb
    return y * _sigmoid(y)


def _inproj_prompt_kernel(x_ref, g_ref, wm_ref, wih_ref, wil_ref, rc_ref, ra_ref, rb_ref,
                          cw_ref, cb_ref, lng_ref, lnb_ref,
                          qt_ref, kt32_ref, vt32_ref, kb_ref, vto_ref, kit32_ref, kib_ref,
                          qit_ref, wit_ref, co_ref, ctail_ref, ubuf, ush):
    i = pl.program_id(1)
    tt = x_ref.shape[1]

    @pl.when(i == 0)
    def _():
        ubuf[0:HALO, :] = jnp.zeros((HALO, ubuf.shape[1]), F32)

    @pl.when(i > 0)
    def _():
        ubuf[0:HALO, :] = ubuf[tt:tt + HALO, :]

    def conv_branch(u):
        ubuf[HALO:HALO + tt, :] = u
        ctail_ref[0] = ubuf[tt:tt + HALO, :]

        span = ush.shape[1]
        for s in range(1, 8):
            ush[s - 1] = ubuf[pl.ds(s, span), :]

        rc_rows = 64
        cb = cb_ref[...]
        lng = lng_ref[...]
        lnb = lnb_ref[...]
        off = HALO - (CONV_WIDTH - 1)
        for r in range(tt // rc_rows):
            acc = jnp.broadcast_to(cb, (rc_rows, cb.shape[1]))
            for j in range(CONV_WIDTH):
                a, s = divmod(off + j, 8)
                rows = pl.ds(8 * a + r * rc_rows, rc_rows)
                win = ubuf[rows, :] if s == 0 else ush[s - 1, rows, :]
                acc = acc + win * cw_ref[j:j + 1, :]
            co_ref[0, r * rc_rows:(r + 1) * rc_rows, :] = _conv_post(acc, lng, lnb).astype(BF16)

    p = _inproj(x_ref[0], g_ref[...], wm_ref, wih_ref, wil_ref, rc_ref[...], ra_ref[...], rb_ref[...], conv_branch)
    for hh in range(ATTN_HEADS):
        qt_ref[0, hh] = p["q"][hh].T.astype(BF16)
    kt32_ref[0] = p["k"].T
    kb_ref[0] = p["k"].astype(BF16)
    vt = p["v"].T
    vt32_ref[0] = vt
    ones = jnp.ones((HEAD_DIM, tt), F32)
    for g in range(N_KV_HEADS):
        vto_ref[0, g] = jnp.concatenate([vt[g * HEAD_DIM:(g + 1) * HEAD_DIM], ones], axis=0).astype(BF16)
    kiw = p["kiw"]
    kib_ref[0] = kiw[:, :IDX_DIM].astype(BF16)
    kiwt = kiw.T
    kit32_ref[0] = kiwt[:IDX_DIM]
    wit_ref[0] = kiwt[IDX_DIM:IDX_DIM + IDX_HEADS]
    for c, chunk in enumerate(p["qic"]):
        ct = chunk.T.astype(BF16)
        for half in range(LANES // IDX_DIM):
            qit_ref[0, c * (LANES // IDX_DIM) + half] = ct[half * IDX_DIM:(half + 1) * IDX_DIM]


def _inproj_sample_kernel(x_ref, g_ref, wm_ref, wih_ref, wil_ref, rc_ref, ra_ref, rb_ref,
                          cw_ref, cb_ref, lng_ref, lnb_ref, st_ref,
                          qpad_ref, k32_ref, v32_ref, kb_ref, vb_ref, ki32_ref, kih_ref, kil_ref,
                          qih_ref, qil_ref, wi_ref, co_ref, cnew_ref, fbuf, uscr):
    hist, db, cch = st_ref.shape
    ds = x_ref.shape[0] // db

    def conv_branch(u):
        nch = cch // LANES
        for cc in range(nch):
            uscr[cc] = u[:, cc * LANES:(cc + 1) * LANES]
        fbuf[0:hist] = st_ref[...]
        for t in range(ds):
            for cc in range(nch):
                fbuf[hist + t, :, cc * LANES:(cc + 1) * LANES] = uscr[cc, pl.ds(t, db, stride=ds), :]
        acc = jnp.broadcast_to(cb_ref[...].reshape(1, 1, cch), (ds, db, cch))
        for j in range(CONV_WIDTH):
            acc = acc + fbuf[j:j + ds] * cw_ref[j:j + 1, :].reshape(1, 1, cch)
        y = _conv_post(acc.reshape(ds * db, cch), lng_ref[...], lnb_ref[...])
        for t in range(ds):
            for cc in range(nch):
                uscr[cc, pl.ds(t, db, stride=ds), :] = y[t * db:(t + 1) * db, cc * LANES:(cc + 1) * LANES]
        for cc in range(nch):
            co_ref[:, cc * LANES:(cc + 1) * LANES] = uscr[cc].astype(BF16)
        cnew_ref[...] = fbuf[ds:ds + hist]

    p = _inproj(x_ref[...], g_ref[...], wm_ref, wih_ref, wil_ref, rc_ref[...], ra_ref[...], rb_ref[...], conv_branch)
    for hh in range(ATTN_HEADS):
        qpad_ref[:, hh * LANES:(hh + 1) * LANES] = p["q"][hh].astype(BF16)
    k32_ref[...] = p["k"]
    v32_ref[...] = p["v"]
    kb_ref[...] = p["k"].astype(BF16)
    vb_ref[...] = p["v"].astype(BF16)
    ki = p["kiw"][:, :IDX_DIM]
    ki32_ref[...] = ki
    kih, kil = _split(ki)
    kih_ref[...] = kih
    kil_ref[...] = kil
    wi_ref[...] = p["kiw"][:, IDX_DIM:IDX_DIM + IDX_HEADS]
    per_chunk = LANES // IDX_DIM
    for hh in range(IDX_HEADS):
        half = hh % per_chunk
        hi, lo = _split(p["qic"][hh // per_chunk][:, half * IDX_DIM:(half + 1) * IDX_DIM])
        qih_ref[hh] = hi
        qil_ref[hh] = lo


def _inproj_prompt(x, g_pre, wm, wih, wil, rc, ra, rb, cw, cb, lng, lnb):
    b, t, d = x.shape
    tt = TOK_TILE
    cch = cw.shape[1]
    full = lambda a: pl.BlockSpec(a.shape, lambda bi, i: (0,) * a.ndim)
    tok = lambda w: pl.BlockSpec((1, tt, w), lambda bi, i: (bi, i, 0))
    feat = lambda w: pl.BlockSpec((1, w, tt), lambda bi, i: (bi, 0, i))
    tab = pl.BlockSpec((tt, LANES), lambda bi, i: (i, 0))
    hfeat = lambda n, w: pl.BlockSpec((1, n, w, tt), lambda bi, i: (bi, 0, 0, i))
    sds = jax.ShapeDtypeStruct
    out_shape = [sds((b, ATTN_HEADS, LANES, t), BF16), sds((b, LANES, t), F32), sds((b, LANES, t), F32),
                 sds((b, t, LANES), BF16), sds((b, N_KV_HEADS, LANES, t), BF16),
                 sds((b, IDX_DIM, t), F32), sds((b, t, IDX_DIM), BF16),
                 sds((b, IDX_HEADS, IDX_DIM, t), BF16),
                 sds((b, IDX_HEADS, t), F32), sds((b, t, cch), BF16), sds((b, HALO, cch), F32)]
    out_specs = [hfeat(ATTN_HEADS, LANES), feat(LANES), feat(LANES), tok(LANES), hfeat(N_KV_HEADS, LANES),
                 feat(IDX_DIM), tok(IDX_DIM), hfeat(IDX_HEADS, IDX_DIM), feat(IDX_HEADS), tok(cch),
                 pl.BlockSpec((1, HALO, cch), lambda bi, i: (bi, 0, 0))]
    return pl.pallas_call(
        _inproj_prompt_kernel,
        grid=(b, t // tt),
        in_specs=[tok(d), full(g_pre), full(wm), full(wih), full(wil), tab, tab, tab,
                  full(cw), full(cb), full(lng), full(lnb)],
        out_specs=out_specs,
        out_shape=out_shape,
        scratch_shapes=[pltpu.VMEM((tt + HALO, cch), F32), pltpu.VMEM((7, tt + HALO - 8, cch), F32)],
        compiler_params=pltpu.CompilerParams(dimension_semantics=("arbitrary", "arbitrary"),
                                             vmem_limit_bytes=VMEM_LIMIT),
        name="inproj_prompt",
    )(x, g_pre, wm, wih, wil, rc, ra, rb, cw, cb, lng, lnb)


def _inproj_sample(x, g_pre, wm, wih, wil, rc, ra, rb, cw, cb, lng, lnb, state):
    n, d = x.shape
    hist, db, cch = state.shape
    ds = n // db
    sds = jax.ShapeDtypeStruct
    out_shape = [sds((n, ATTN_HEADS * LANES), BF16), sds((n, LANES), F32), sds((n, LANES), F32),
                 sds((n, LANES), BF16), sds((n, LANES), BF16),
                 sds((n, IDX_DIM), F32), sds((n, IDX_DIM), BF16), sds((n, IDX_DIM), BF16),
                 sds((IDX_HEADS, n, IDX_DIM), BF16), sds((IDX_HEADS, n, IDX_DIM), BF16),
                 sds((n, IDX_HEADS), F32), sds((n, cch), BF16), sds((hist, db, cch), F32)]
    return pl.pallas_call(
        _inproj_sample_kernel,
        out_shape=out_shape,
        scratch_shapes=[pltpu.VMEM((hist + ds, db, cch), F32), pltpu.VMEM((cch // LANES, n, LANES), F32)],
        compiler_params=pltpu.CompilerParams(vmem_limit_bytes=VMEM_LIMIT),
        name="inproj_sample",
    )(x, g_pre, wm, wih, wil, rc, ra, rb, cw, cb, lng, lnb, state)


def _key_to_f32(key):
    bits = jnp.where(key < 0, key ^ INT_MAX, key)
    return lax.bitcast_convert_type(bits, F32)


def _row_reduce(sources, rows, fn, op, init):
    c = jnp.full((rows, LANES), init, F32)
    for get, nblk, width, col0 in sources:
        def body(j, c, get=get, width=width, col0=col0):
            blk = get(j)
            col = col0 + j * width + lax.broadcasted_iota(I32, blk.shape, 1)
            val = fn(blk, col)
            for cc in range(width // LANES):
                c = op(c, val[:, cc * LANES:(cc + 1) * LANES])
            return c
        c = lax.fori_loop(0, nblk, body, c)
    return c


def _fold_rows(x, op, part):
    groups = [x[r:r + part] for r in range(0, x.shape[0], part)]
    parts = groups[:FOLD_CHAINS]
    for k, g in enumerate(groups[FOLD_CHAINS:]):
        parts[k % FOLD_CHAINS] = op(parts[k % FOLD_CHAINS], g)
    while len(parts) > 1:
        parts = [op(parts[k], parts[k + 1]) if k + 1 < len(parts) else parts[k] for k in range(0, len(parts), 2)]
    return parts[0]


class _KeyMajor(NamedTuple):
    get: Callable
    nblk: Any
    q: int


def _vshape(sources, rows):
    return (1, sources.q) if isinstance(sources, _KeyMajor) else (rows, 1)


def _scan(sources, rows, fn, op, red, init):
    if not isinstance(sources, _KeyMajor):
        return red(_row_reduce(sources, rows, fn, op, init), axis=1, keepdims=True)
    def body(j, c):
        for blk, key0 in sources.get(j):
            key = key0 + lax.broadcasted_iota(I32, blk.shape, 0)
            c = op(c, _fold_rows(fn(blk, key), op, 8))
        return c

    c = lax.fori_loop(0, sources.nblk, body, jnp.full((8, sources.q), init, F32))
    return red(c, axis=0, keepdims=True)


def _count(sources, rows, pred):
    return _scan(sources, rows, lambda blk, col: jnp.where(pred(blk, col), 1.0, 0.0), jnp.add, jnp.sum, 0.0)


def _topk_select(sources, rows, topk, vmin, vmax, nvis, cpos, cnn):
    kf = float(topk)
    few = nvis <= kf
    zero_top = (cpos <= kf) & (cnn >= kf) & jnp.logical_not(few)
    up = cpos > kf

    def unsettled(carry):
        return (carry[0] < BISECT_STEPS) & (jnp.min(carry[6]) < 0.5)

    logit = lambda c: jnp.log((c + 0.5) / (nvis - c + 0.5))
    z_k = logit(kf)

    def step(carry):
        it, lo, hi, clo, chi, thr, done = carry
        z_lo, z_hi = logit(clo), logit(chi)
        den = z_lo - z_hi
        t = jnp.clip(jnp.where(den > 0.0, (z_lo - z_k) / den, 0.5), 1.0 / 32.0, 31.0 / 32.0)
        t = jnp.where((it & 3) == 3, 0.5, t)
        mid = lo + (hi - lo) * t
        c = _count(sources, rows, lambda blk, col: blk > mid)
        hit = (c == kf) & (done < 0.5)
        above, below = c > kf, c < kf
        return (it + 1, jnp.where(above, mid, lo), jnp.where(below, mid, hi),
                jnp.where(above, c, clo), jnp.where(below, c, chi),
                jnp.where(hit, mid, thr), jnp.where(hit, 1.0, done))

    carry0 = (jnp.int32(0), jnp.where(up, 0.0, vmin), jnp.where(up, vmax, 0.0),
              jnp.where(up, cpos, nvis), jnp.where(up, 0.0, cpos),
              jnp.where(few, -jnp.inf, 0.0), jnp.where(few | zero_top, 1.0, 0.0))
    thr, done = lax.while_loop(unsettled, step, carry0)[5:]
    settled = done > 0.5
    zero_need = kf - cpos
    cidx = jnp.where(zero_top & (zero_need > 0.0), INT_MAX, -1)
    need = jnp.where(zero_top, zero_need, 0.0)
    cut = jnp.where(zero_top & (cnn > kf) & (zero_need > 0.0), 1.0, 0.0)

    def exact(_):
        thr_e, cnt_e = _kth_largest_exact(sources, rows, topk)
        need_e = kf - _count(sources, rows, lambda blk, col: blk > thr_e)
        fin = thr_e > -jnp.inf
        return (jnp.where(settled, thr, thr_e), jnp.where(settled, cidx, jnp.where(fin, INT_MAX, -1)),
                jnp.where(settled, need, need_e), jnp.where(settled, cut, jnp.where(fin & (cnt_e > kf), 1.0, 0.0)))

    return lax.cond(jnp.min(done) < 0.5, exact, lambda _: (thr, cidx, need, cut), 0)


def _tie_cutoffs(slab_sources, nrows, ncols_max, thr_ref, need_ref, cut_ref, cidx_ref):
    n_it = int(np.ceil(np.log2(ncols_max + 1))) + 1

    def slab(sl, _):
        rs = pl.ds(pl.multiple_of(sl * 8, 8), 8)
        cut = cut_ref[rs, :] > 0.5

        @pl.when(jnp.max(cut_ref[rs, :]) > 0.5)
        def _():
            thr = thr_ref[rs, :]
            need = need_ref[rs, :]
            src = slab_sources(pl.multiple_of(sl * 8, 8))

            def bis_c(_, carry):
                lo_c, hi_c = carry
                mid = (lo_c + hi_c) >> 1
                ge = _count(src, 8, lambda blk, col: (blk == thr) & (col <= mid)) >= need
                return jnp.where(ge, lo_c, mid), jnp.where(ge, mid, hi_c)

            _, hi_c = lax.fori_loop(0, n_it, bis_c, (jnp.full((8, 1), -1, I32),
                                                     jnp.full((8, 1), ncols_max - 1, I32)))
            cidx_ref[rs, :] = jnp.where(cut, hi_c, cidx_ref[rs, :])
        return 0

    lax.fori_loop(0, nrows // 8, slab, 0)


def _drop_excess_ties(get_blk, put_blk, nblk, width, q, thr, need, cut):
    tri = jnp.where(lax.broadcasted_iota(I32, (width, width), 1) <= lax.broadcasted_iota(I32, (width, width), 0),
                    1.0, 0.0).astype(BF16)
    marked = cut > 0.5

    def body(j, before):
        blk = get_blk(j)
        tied = (blk == thr) & marked
        rank = _dot(tri, jnp.where(tied, 1.0, 0.0).astype(BF16))
        put_blk(j, jnp.where(tied & (before + rank > need), -jnp.inf, blk))
        return before + jnp.max(rank, axis=0, keepdims=True)

    lax.fori_loop(0, nblk, body, jnp.zeros((1, q), F32))


def _kth_largest_exact(sources, rows, topk):
    kf = float(topk)

    def bis(_, carry):
        lo, hi = carry
        mid = (lo >> 1) + (hi >> 1) + (lo & hi & 1)
        thr = _key_to_f32(mid)
        ge = _count(sources, rows, lambda blk, col: blk >= thr) >= kf
        return jnp.where(ge, mid, lo), jnp.where(ge, hi, mid)

    lo0 = jnp.full(_vshape(sources, rows), KEY_NEG_INF, I32)
    hi0 = jnp.full(_vshape(sources, rows), KEY_NAN_HI, I32)
    _, hi = lax.fori_loop(0, 32, bis, (lo0, hi0))

    def below(upper):
        m = _scan(sources, rows, lambda blk, col: jnp.where(blk < upper, blk, -jnp.inf),
                  jnp.maximum, jnp.max, -jnp.inf)
        return m, _count(sources, rows, lambda blk, col: blk >= m)

    def short(carry):
        return jnp.max(jnp.where(carry[2] < kf, 1.0, 0.0)) > 0.0

    def lower(carry):
        upper, m, c = carry
        upper = jnp.where(c < kf, m, upper)
        m, c = below(upper)
        return upper, m, c

    upper0 = _key_to_f32(hi)
    m0, c0 = below(upper0)
    _, thr, cnt = lax.while_loop(short, lower, (upper0, m0, c0))
    return thr, cnt


def _selected(blk, col, thr, cidx):
    return (blk > thr) | ((blk == thr) & (col <= cidx))


def _attn_prompt_kernel(qt_ref, qit_ref, wit_ref, kib_ref, kb_ref, vto_ref, o_ref,
                        sc_ref, vmx_ref, vmn_ref, cp_ref, cn_ref, thr_ref, cidx_ref, m_ref, acc_ref, *, topk):
    i = pl.program_id(1)
    tq = qt_ref.shape[3]
    tk = sc_ref.shape[1]
    nkb = (i * tq + tq + tk - 1) // tk
    q_pos = i * tq + lax.broadcasted_iota(I32, (tk, tq), 1)
    key_of = lambda j: j * tk + lax.broadcasted_iota(I32, (tk, tq), 0)
    wit = wit_ref[0]

    vmx_ref[...] = jnp.full(vmx_ref.shape, -jnp.inf, F32)
    vmn_ref[...] = jnp.full(vmn_ref.shape, jnp.inf, F32)
    cp_ref[...] = jnp.zeros(cp_ref.shape, F32)
    cn_ref[...] = jnp.zeros(cn_ref.shape, F32)

    def score_block(j, _):
        kib = kib_ref[0, pl.ds(pl.multiple_of(j * tk, tk), tk), :]
        acc = jnp.zeros((tk, tq), F32)
        for hh in range(IDX_HEADS):
            acc = acc + jnp.maximum(_dot(kib, qit_ref[0, hh]), 0.0) * wit[hh:hh + 1, :]
        vis = key_of(j) <= q_pos
        sc = jnp.where(vis, acc, -jnp.inf)
        sc_ref[j] = sc
        vmx_ref[...] = jnp.maximum(vmx_ref[...], _fold_rows(sc, jnp.maximum, 8))
        vmn_ref[...] = jnp.minimum(vmn_ref[...], _fold_rows(jnp.where(vis, acc, jnp.inf), jnp.minimum, 8))
        cp_ref[...] = cp_ref[...] + _fold_rows(jnp.where(sc > 0.0, 1.0, 0.0), jnp.add, 8)
        cn_ref[...] = cn_ref[...] + _fold_rows(jnp.where(sc >= 0.0, 1.0, 0.0), jnp.add, 8)
        return 0

    lax.fori_loop(0, nkb, score_block, 0)

    @pl.when(nkb < sc_ref.shape[0])
    def _():
        sc_ref[nkb] = jnp.full((tk, tq), -jnp.inf, F32)

    pairs = _KeyMajor(lambda j: [(sc_ref[2 * j], 2 * j * tk), (sc_ref[2 * j + 1], (2 * j + 1) * tk)],
                      (nkb + 1) // 2, tq)
    nvis = (i * tq + 1 + lax.broadcasted_iota(I32, (1, tq), 1)).astype(F32)
    thr, cidx, need, cut = _topk_select(
        pairs, tq, topk, jnp.min(vmn_ref[...], axis=0, keepdims=True), jnp.max(vmx_ref[...], axis=0, keepdims=True),
        nvis, jnp.sum(cp_ref[...], axis=0, keepdims=True), jnp.sum(cn_ref[...], axis=0, keepdims=True))
    thr_ref[...] = thr
    cidx_ref[...] = cidx

    for half in range(tq // LANES):
        qs = slice(half * LANES, (half + 1) * LANES)

        @pl.when(jnp.max(cut[:, qs]) > 0.5)
        def _(qs=qs):
            def put(j, val):
                sc_ref[j, :, qs] = val
            _drop_excess_ties(lambda j: sc_ref[j, :, qs], put, nkb, tk, LANES, thr[:, qs], need[:, qs], cut[:, qs])

    m_ref[...] = jnp.full(m_ref.shape, NEG, F32)
    acc_ref[...] = jnp.zeros(acc_ref.shape, F32)
    thr = thr_ref[...]
    cidx = cidx_ref[...]

    def attn_block(j, _):
        ks = pl.ds(pl.multiple_of(j * tk, tk), tk)
        bias = jnp.where(_selected(sc_ref[j], key_of(j), thr, cidx), 0.0, NEG).astype(BF16)
        kb = kb_ref[0, ks, :]
        qk = lambda hh: _dot(kb, qt_ref[0, hh])
        s_next = qk(0)
        pending = None
        for hh in range(ATTN_HEADS + 1):
            if hh < ATTN_HEADS:
                s = s_next.astype(BF16) + bias
                if hh + 1 < ATTN_HEADS:
                    s_next = qk(hh + 1)
                blk_max = jnp.max(_fold_rows(s, jnp.maximum, 16).astype(F32), axis=0, keepdims=True)
                m_old = m_ref[hh]
                m_new = jnp.maximum(m_old, blk_max)
                alpha = jnp.exp2(m_old - m_new)
                p = jnp.exp2(s - m_new.astype(BF16))
                m_ref[hh] = m_new
            if pending is not None:
                ph, palpha, pp = pending
                acc_ref[ph] = palpha * acc_ref[ph] + _dot(vto_ref[0, ph // KV_GROUP, :, ks], pp)
            pending = (hh, alpha, p) if hh < ATTN_HEADS else None
        return 0

    lax.fori_loop(0, nkb, attn_block, 0)

    outs = []
    for hh in range(ATTN_HEADS):
        acc = acc_ref[hh]
        outs.append((acc[:HEAD_DIM] / acc[HEAD_DIM:]).T)
    o_ref[0] = jnp.concatenate(outs, axis=1).astype(BF16)


def _attn_prompt(qt, qit, wit, kib, kb, vto, topk):
    b, _, _, t = qt.shape
    tq = Q_TILE
    tk = K_BLOCK
    assert t % tq == 0 and t % tk == 0 and tk % tq == 0
    hfeat = lambda n, w: pl.BlockSpec((1, n, w, tq), lambda bi, i: (bi, 0, 0, i))
    seq = lambda w: pl.BlockSpec((1, t, w), lambda bi, i: (bi, 0, 0))
    return pl.pallas_call(
        functools.partial(_attn_prompt_kernel, topk=topk),
        grid=(b, t // tq),
        in_specs=[hfeat(ATTN_HEADS, LANES), hfeat(IDX_HEADS, IDX_DIM),
                  pl.BlockSpec((1, IDX_HEADS, tq), lambda bi, i: (bi, 0, i)), seq(IDX_DIM), seq(LANES),
                  pl.BlockSpec((1, N_KV_HEADS, LANES, t), lambda bi, i: (bi, 0, 0, 0))],
        out_specs=pl.BlockSpec((1, tq, ATTN_WIDTH), lambda bi, i: (bi, i, 0)),
        out_shape=jax.ShapeDtypeStruct((b, t, ATTN_WIDTH), BF16),
        scratch_shapes=[pltpu.VMEM((t // tk + (t // tk) % 2, tk, tq), F32)] + [pltpu.VMEM((8, tq), F32)] * 4
                       + [pltpu.VMEM((1, tq), F32), pltpu.VMEM((1, tq), I32),
                          pltpu.VMEM((ATTN_HEADS, 1, tq), F32), pltpu.VMEM((ATTN_HEADS, LANES, tq), F32)],
        compiler_params=pltpu.CompilerParams(dimension_semantics=("arbitrary", "arbitrary"),
                                             vmem_limit_bytes=VMEM_LIMIT),
        name="attn_prompt",
    )(qt, qit, wit, kib, kb, vto)


def _score_sample_kernel(pt_ref, qih_ref, qil_ref, wcol_ref, knh_ref, knl_ref, *rest):
    pages = rest[:PAGES_PER_STEP]
    scp_ref, scn_ref, kh_s, kl_s = rest[PAGES_PER_STEP:]
    c = pl.program_id(1)
    ds = qih_ref.shape[1]
    page = pages[0].shape[1]
    qh = qih_ref[...].reshape(IDX_HEADS * ds, IDX_DIM)
    ql = qil_ref[...].reshape(IDX_HEADS * ds, IDX_DIM)
    wcol = wcol_ref[0]

    def combine(s):
        r = jnp.maximum(s, 0.0) * wcol
        return jnp.sum(r.reshape(IDX_HEADS, ds, r.shape[1]), axis=0)

    for pg in range(PAGES_PER_STEP):
        kh, kl = _split(pages[pg][...])
        kh_s[:, pg * page:(pg + 1) * page] = kh
        kl_s[:, pg * page:(pg + 1) * page] = kl
    kh = kh_s[...]
    scp_ref[0] = combine(_dot(qh, kh) + _dot(ql, kh) + _dot(qh, kl_s[...]))

    @pl.when(c == pl.num_programs(1) - 1)
    def _():
        knh = knh_ref[0]
        sn = combine(_dot_nt(qh, knh) + _dot_nt(ql, knh) + _dot_nt(qh, knl_ref[0]))
        tok = lax.broadcasted_iota(I32, sn.shape, 0)
        col = lax.broadcasted_iota(I32, sn.shape, 1)
        scn_ref[0] = jnp.where(col <= tok, sn, -jnp.inf)


def _select_sample_kernel(scp_ref, scn_ref, thr_ref, cidx_ref, need_ref, cut_ref, *, topk, blk_w):
    rows, past = scp_ref.shape

    def sources(rs):
        return [(lambda j: scp_ref[rs, pl.ds(pl.multiple_of(j * blk_w, blk_w), blk_w)], past // blk_w, blk_w, 0),
                (lambda j: scn_ref[rs, :], 1, scn_ref.shape[1], past)]

    src = sources(slice(None))
    lane_red = lambda fn, op, init, red: red(_row_reduce(src, rows, fn, op, init), axis=1, keepdims=True)
    vmax = lane_red(lambda blk, col: blk, jnp.maximum, -jnp.inf, jnp.max)
    vmin = lane_red(lambda blk, col: jnp.where(blk == -jnp.inf, jnp.inf, blk), jnp.minimum, jnp.inf, jnp.min)
    nvis = _count(src, rows, lambda blk, col: blk > -jnp.inf)
    cpos = _count(src, rows, lambda blk, col: blk > 0.0)
    cnn = _count(src, rows, lambda blk, col: blk >= 0.0)
    thr, cidx, need, cut = _topk_select(src, rows, topk, vmin, vmax, nvis, cpos, cnn)
    thr_ref[...] = thr
    cidx_ref[...] = cidx
    need_ref[...] = need
    cut_ref[...] = cut
    _tie_cutoffs(lambda r0: sources(pl.ds(r0, 8)), rows, past + scn_ref.shape[1],
                 thr_ref, need_ref, cut_ref, cidx_ref)


def _attn_sample_kernel(pt_ref, qpad_ref, scp_ref, scn_ref, thr_ref, cidx_ref, knew_ref, vnew_ref, *rest,
                        past_len):
    kpages = rest[:PAGES_PER_STEP]
    vpages = rest[PAGES_PER_STEP:2 * PAGES_PER_STEP]
    o_ref, kt_s, vt_s, m_ref, l_ref, acc_ref = rest[2 * PAGES_PER_STEP:]
    c = pl.program_id(1)
    ds = qpad_ref.shape[0]
    page = kpages[0].shape[1]
    thr = thr_ref[...]
    cidx = cidx_ref[...]

    @pl.when(c == 0)
    def _():
        m_ref[...] = jnp.full(m_ref.shape, NEG, F32)
        l_ref[...] = jnp.zeros(l_ref.shape, F32)
        acc_ref[...] = jnp.zeros(acc_ref.shape, F32)

    q_all = jnp.concatenate([qpad_ref[:, hh * LANES:(hh + 1) * LANES] for hh in range(ATTN_HEADS)], axis=0)

    def attend(sc, col, s_of_q, pv):
        bias1 = jnp.where(_selected(sc, col, thr, cidx), 0.0, NEG)
        s = s_of_q + jnp.concatenate([bias1] * ATTN_HEADS, axis=0)
        m_old = m_ref[...]
        m_new = jnp.maximum(m_old, jnp.max(s, axis=1, keepdims=True))
        alpha = jnp.exp2(m_old - m_new)
        p = jnp.exp2(s - m_new)
        l_ref[...] = alpha * l_ref[...] + jnp.sum(p, axis=1, keepdims=True)
        acc_ref[...] = alpha * acc_ref[...] + pv(p.astype(BF16))
        m_ref[...] = m_new

    for pg in range(PAGES_PER_STEP):
        kt_s[:, pg * page:(pg + 1) * page] = kpages[pg][...].astype(BF16)
        vt_s[:, pg * page:(pg + 1) * page] = vpages[pg][...].astype(BF16)
    width = PAGES_PER_STEP * page
    col = c * width + lax.broadcasted_iota(I32, (ds, width), 1)
    attend(scp_ref[0], col, _dot(q_all, kt_s[...]), lambda p: _dot_nt(p, vt_s[...]))

    @pl.when(c == pl.num_programs(1) - 1)
    def _():
        sc = scn_ref[0]
        ncol = past_len + lax.broadcasted_iota(I32, sc.shape, 1)
        attend(sc, ncol, _dot_nt(q_all, knew_ref[0]), lambda p: _dot(p, vnew_ref[0]))
        o = acc_ref[...] / l_ref[...]
        outs = []
        for hh in range(ATTN_HEADS):
            g = hh // KV_GROUP
            outs.append(o[hh * ds:(hh + 1) * ds, g * HEAD_DIM:(g + 1) * HEAD_DIM])
        o_ref[...] = jnp.concatenate(outs, axis=1).astype(BF16)


def _attn_sample(qpad, qih, qil, wi, kih, kil, kb, vb, pool_kt, pool_vt, pool_ikt, page_table, topk, db):
    n = qpad.shape[0]
    ds = n // db
    n_pool, _, page = pool_ikt.shape
    n_pages = page_table.shape[1]
    past_len = n_pages * page
    pps = PAGES_PER_STEP
    n_chunks = n_pages // pps
    assert n_pages % pps == 0 and ds <= LANES

    wcol = wi.reshape(db, ds, IDX_HEADS).transpose(0, 2, 1).reshape(db, IDX_HEADS * ds, 1)
    padk = lambda a: jnp.pad(a.reshape(db, ds, a.shape[1]), ((0, 0), (0, LANES - ds), (0, 0)))
    knh, knl, knew, vnew = padk(kih), padk(kil), padk(kb), padk(vb)

    def page_spec(w, pg):
        return pl.BlockSpec((None, w, page), lambda b, c, pt: (pt[b, c * pps + pg], 0, 0))

    hm = pl.BlockSpec((IDX_HEADS, ds, IDX_DIM), lambda b, c, pt: (0, b, 0))
    seq3 = lambda r, w: pl.BlockSpec((1, r, w), lambda b, c, pt: (b, 0, 0))
    chunk = pl.BlockSpec((1, ds, pps * page), lambda b, c, pt: (b, 0, c))
    scp, scn = pl.pallas_call(
        _score_sample_kernel,
        grid_spec=pltpu.PrefetchScalarGridSpec(
            num_scalar_prefetch=1, grid=(db, n_chunks),
            in_specs=[hm, hm, seq3(IDX_HEADS * ds, 1), seq3(LANES, IDX_DIM), seq3(LANES, IDX_DIM)]
                     + [page_spec(IDX_DIM, pg) for pg in range(pps)],
            out_specs=[chunk, seq3(ds, LANES)],
            scratch_shapes=[pltpu.VMEM((IDX_DIM, pps * page), BF16), pltpu.VMEM((IDX_DIM, pps * page), BF16)]),
        out_shape=[jax.ShapeDtypeStruct((db, ds, past_len), F32), jax.ShapeDtypeStruct((db, ds, LANES), F32)],
        compiler_params=pltpu.CompilerParams(dimension_semantics=("arbitrary", "arbitrary"),
                                             vmem_limit_bytes=VMEM_LIMIT),
        name="score_sample",
    )(page_table, qih, qil, wcol, knh, knl, *([pool_ikt] * pps))

    rows = min(SEL_ROWS, n)
    blk_w = 512
    thr, cidx = pl.pallas_call(
        functools.partial(_select_sample_kernel, topk=topk, blk_w=blk_w),
        grid=(n // rows,),
        in_specs=[pl.BlockSpec((rows, past_len), lambda r: (r, 0)), pl.BlockSpec((rows, LANES), lambda r: (r, 0))],
        out_specs=[pl.BlockSpec((rows, 1), lambda r: (r, 0)), pl.BlockSpec((rows, 1), lambda r: (r, 0))],
        out_shape=[jax.ShapeDtypeStruct((n, 1), F32), jax.ShapeDtypeStruct((n, 1), I32)],
        scratch_shapes=[pltpu.VMEM((rows, 1), F32), pltpu.VMEM((rows, 1), F32)],
        compiler_params=pltpu.CompilerParams(dimension_semantics=("arbitrary",), vmem_limit_bytes=VMEM_LIMIT),
        name="select_sample",
    )(scp.reshape(n, past_len), scn.reshape(n, LANES))

    tokrows = lambda w: pl.BlockSpec((ds, w), lambda b, c, pt: (b, 0))
    nrow = ATTN_HEADS * ds
    return pl.pallas_call(
        functools.partial(_attn_sample_kernel, past_len=past_len),
        grid_spec=pltpu.PrefetchScalarGridSpec(
            num_scalar_prefetch=1, grid=(db, n_chunks),
            in_specs=[tokrows(ATTN_HEADS * LANES), chunk, seq3(ds, LANES),
                      tokrows(1), tokrows(1), seq3(LANES, LANES), seq3(LANES, LANES)]
                     + [page_spec(LANES, pg) for pg in range(pps)] * 2,
            out_specs=tokrows(ATTN_WIDTH),
            scratch_shapes=[pltpu.VMEM((LANES, pps * page), BF16), pltpu.VMEM((LANES, pps * page), BF16),
                            pltpu.VMEM((nrow, 1), F32), pltpu.VMEM((nrow, 1), F32),
                            pltpu.VMEM((nrow, LANES), F32)]),
        out_shape=jax.ShapeDtypeStruct((n, ATTN_WIDTH), BF16),
        compiler_params=pltpu.CompilerParams(dimension_semantics=("arbitrary", "arbitrary"),
                                             vmem_limit_bytes=VMEM_LIMIT),
        name="attn_sample",
    )(page_table, qpad, scp, scn, thr, cidx, knew, vnew, *([pool_kt] * pps), *([pool_vt] * pps))


def _mix_mlp_kernel(a_ref, c_ref, x_ref, woa_ref, woc_ref, g1_ref, g2_ref, wup_ref, wdn_ref, g3_ref, y_ref):
    m = _dot(a_ref[...], woa_ref[...]) + _dot(c_ref[...], woc_ref[...])
    x1 = x_ref[...] + _rms(m) * g1_ref[...]
    h = (_rms(x1) * g2_ref[...]).astype(BF16)
    dff = wup_ref.shape[1]
    f = jnp.zeros(x1.shape, F32)
    for cc in range(dff // FF_CHUNK):
        cs = slice(cc * FF_CHUNK, (cc + 1) * FF_CHUNK)
        up = jnp.maximum(_dot(h, wup_ref[:, cs]), 0.0)
        f = f + _dot((up * up).astype(BF16), wdn_ref[cs, :])
    y_ref[...] = x1 + _rms(f) * g3_ref[...]


def _mix_mlp(a, c, x, woa, woc, g1, g2, wup, wdn, g3):
    n, d = x.shape
    tm = min(MLP_TILE, n)
    full = lambda w: pl.BlockSpec(w.shape, lambda i: (0,) * w.ndim)
    tok = lambda w: pl.BlockSpec((tm, w), lambda i: (i, 0))
    return pl.pallas_call(
        _mix_mlp_kernel,
        grid=(n // tm,),
        in_specs=[tok(a.shape[1]), tok(c.shape[1]), tok(d), full(woa), full(woc), full(g1), full(g2),
                  full(wup), full(wdn), full(g3)],
        out_specs=tok(d),
        out_shape=jax.ShapeDtypeStruct((n, d), F32),
        compiler_params=pltpu.CompilerParams(dimension_semantics=("arbitrary",), vmem_limit_bytes=VMEM_LIMIT),
        name="mix_mlp",
    )(a, c, x, woa, woc, g1, g2, wup, wdn, g3)


def _rope_tables(pos):
    rot = HEAD_DIM // 4
    half = rot // 2
    inv_freq = jnp.power(ROPE_THETA, -jnp.arange(half, dtype=F32) * (2.0 / rot))
    dim = np.arange(LANES) % HEAD_DIM
    ang = pos.astype(F32)[:, None] * inv_freq[dim % half][None, :]
    cos, sin = jnp.cos(ang), jnp.sin(ang)
    rc = jnp.where((dim < rot)[None, :], cos, 1.0)
    ra = jnp.where(((dim >= half) & (dim < rot))[None, :], sin, 0.0)
    rb = jnp.where((dim < half)[None, :], -sin, 0.0)
    return rc, ra, rb


def _prep_w_in(w):
    d = w.shape[0]
    sizes = (ATTN_WIDTH, N_KV_HEADS * HEAD_DIM, N_KV_HEADS * HEAD_DIM, IDX_HEADS * IDX_DIM, IDX_DIM, IDX_HEADS)
    offs = np.cumsum((0,) + sizes)
    wq, wk, wv, wqi, wki, wwi = [w[:, offs[i]:offs[i + 1]] for i in range(6)]
    wu = w[:, offs[6]:]
    zpad = jnp.zeros((d, HEAD_DIM), w.dtype)
    qcols = []
    for hh in range(ATTN_HEADS):
        piece = wq[:, hh * HEAD_DIM:(hh + 1) * HEAD_DIM]
        qcols += [piece, zpad] if hh // KV_GROUP == 0 else [zpad, piece]
    wm = jnp.concatenate(qcols + [wk, wv, wu], axis=1).astype(BF16)
    widx = jnp.concatenate([wqi, wki, wwi, jnp.zeros((d, LANES - IDX_DIM - IDX_HEADS), w.dtype)], axis=1)
    wih, wil = _split(widx)
    return wm, wih, wil


def kernel(x_prompt, x_sample, cache_k, cache_v, cache_idx_k, state_conv, page_table, norm_mix_pre, w_in,
           conv_dw_w, conv_dw_b, conv_ln_g, conv_ln_b, w_out, norm_mix_post, norm_mlp_pre, w_up, w_down,
           norm_mlp_post):
    depth = w_in.shape[0]
    b, t, d = x_prompt.shape
    db, ds, _ = x_sample.shape
    n_pool, page = cache_k.shape[1], cache_k.shape[2]
    past_len = page_table.shape[1] * page
    topk_p = min(INDEX_TOPK, t // 4)
    topk_s = min(INDEX_TOPK, (past_len + ds) // 4)
    hist = CONV_WIDTH - 1

    rcp, rap, rbp = _rope_tables(jnp.arange(t, dtype=I32))
    rcs, ras, rbs = [jnp.tile(a, (db, 1)) for a in _rope_tables(past_len + jnp.arange(ds, dtype=I32))]

    xp = x_prompt
    xs = x_sample.reshape(db * ds, d)
    outs = [[] for _ in range(8)]
    row = lambda a: a.reshape(1, -1)
    for l in range(depth):
        wm, wih, wil = _prep_w_in(w_in[l])
        conv_args = (conv_dw_w[l], row(conv_dw_b[l]), row(conv_ln_g[l]), row(conv_ln_b[l]))
        woa = w_out[l, :ATTN_WIDTH].astype(BF16)
        woc = w_out[l, ATTN_WIDTH:].astype(BF16)
        mlp_args = (woa, woc, row(norm_mix_post[l]), row(norm_mlp_pre[l]), w_up[l].astype(BF16),
                    w_down[l].astype(BF16), row(norm_mlp_post[l]))

        (qt, kt32, vt32, kb, vto, kit32, kib, qit, wit, co, ctail) = _inproj_prompt(
            xp, row(norm_mix_pre[l]), wm, wih, wil, rcp, rap, rbp, *conv_args)
        ao = _attn_prompt(qt, qit, wit, kib, kb, vto, topk_p)
        xp = _mix_mlp(ao.reshape(b * t, -1), co.reshape(b * t, -1), xp.reshape(b * t, d), *mlp_args).reshape(b, t, d)
        unt = lambda a: a.reshape(b, N_KV_HEADS, HEAD_DIM, t).transpose(0, 3, 1, 2)
        outs[0].append(unt(kt32))
        outs[1].append(unt(vt32))
        outs[2].append(kit32.transpose(0, 2, 1))
        outs[3].append(ctail[:, HALO - hist:])

        pool_kt = cache_k[l].transpose(0, 2, 3, 1).reshape(n_pool, N_KV_HEADS * HEAD_DIM, page)
        pool_vt = cache_v[l].transpose(0, 2, 3, 1).reshape(n_pool, N_KV_HEADS * HEAD_DIM, page)
        pool_ikt = cache_idx_k[l].transpose(0, 2, 1)
        (qpad, k32, v32, kb, vb, ki32, kih, kil, qih, qil, wi, co, cnew) = _inproj_sample(
            xs, row(norm_mix_pre[l]), wm, wih, wil, rcs, ras, rbs, *conv_args, state_conv[l].transpose(1, 0, 2))
        ao = _attn_sample(qpad, qih, qil, wi, kih, kil, kb, vb, pool_kt, pool_vt, pool_ikt,
                          page_table, topk_s, db)
        xs = _mix_mlp(ao, co, xs, *mlp_args)
        outs[4].append(k32.reshape(db, ds, N_KV_HEADS, HEAD_DIM))
        outs[5].append(v32.reshape(db, ds, N_KV_HEADS, HEAD_DIM))
        outs[6].append(ki32.reshape(db, ds, IDX_DIM))
        outs[7].append(cnew.transpose(1, 0, 2))

    st = [jnp.stack(o, 0) for o in outs]
    return (xp, xs.reshape(db, ds, d), st[0], st[1], st[2], st[3], st[4], st[5], st[6], st[7])
```

```python
import functools
from typing import Any, Callable, NamedTuple

import numpy as np
import jax
import jax.numpy as jnp
from jax import lax
from jax.experimental import pallas as pl
from jax.experimental.pallas import tpu as pltpu

F32 = jnp.float32
BF16 = jnp.bfloat16
I32 = jnp.int32

ATTN_HEADS = 8
HEAD_DIM = 64
ATTN_WIDTH = ATTN_HEADS * HEAD_DIM
N_KV_HEADS = 2
KV_GROUP = ATTN_HEADS // N_KV_HEADS
IDX_HEADS = 8
IDX_DIM = 64
INDEX_TOPK = 256
ROPE_THETA = 500000.0
CONV_WIDTH = 31
RMS_EPS = 1e-6
LN_EPS = 1e-5
SCALE = 0.125
assert HEAD_DIM ** -0.5 == SCALE and IDX_DIM ** -0.5 == SCALE
assert N_KV_HEADS * HEAD_DIM == 128

LANES = 128
LOG2E = 1.4426950408889634
NEG = -(2.0 ** 100)
HALO = 32
TOK_TILE = 256
Q_TILE = 256
K_BLOCK = 512
SEL_ROWS = 128
PV_DELAY = 2
QK_AHEAD = 3
FOLD_CHAINS = 4
BISECT_STEPS = 28
MLP_TILE = 256
FF_CHUNK = 1024
PAGES_PER_STEP = 32
VMEM_LIMIT = 56 * 1024 * 1024

KEY_NEG_INF = -2139095041
KEY_NAN_HI = 2139095041
INT_MAX = 2147483647


def _dot(a, b):
    return jnp.dot(a, b, preferred_element_type=F32)


def _dot_nt(a, b):
    return lax.dot_general(a, b, (((1,), (1,)), ((), ())), preferred_element_type=F32)


def _split(x):
    hi = x.astype(BF16)
    lo = (x - hi.astype(F32)).astype(BF16)
    return hi, lo


def _rms(x):
    return x * lax.rsqrt(jnp.mean(x * x, axis=-1, keepdims=True) + RMS_EPS)


def _sigmoid(x):
    return 1.0 / (1.0 + jnp.exp(-x))


def _inproj(x, g_pre, wm_ref, wih_ref, wil_ref, rc, ra, rb, on_glu):
    h = _rms(x) * g_pre
    h_hi, h_lo = _split(h)
    nq = ATTN_HEADS * LANES
    nqkv = nq + 2 * LANES
    cch = wm_ref.shape[1] - nqkv
    zu = _dot(h_hi, wm_ref[:, nqkv:])
    u = zu[:, :cch // 2] * _sigmoid(zu[:, cch // 2:])
    on_glu(u)
    zm = _dot(h_hi, wm_ref[:, :nqkv])
    wih = wih_ref[...]
    zi = _dot(h_hi, wih) + _dot(h_lo, wih) + _dot(h_hi, wil_ref[...])

    def rope(c):
        return c * rc + pltpu.roll(c, 8, 1) * ra + pltpu.roll(c, LANES - 8, 1) * rb

    q = [rope(zm[:, c * LANES:(c + 1) * LANES]) * (SCALE * LOG2E) for c in range(ATTN_HEADS)]
    k = rope(zm[:, nq:nq + LANES])
    v = zm[:, nq + LANES:nq + 2 * LANES]
    qic = [rope(zi[:, c * LANES:(c + 1) * LANES]) * SCALE for c in range(IDX_HEADS * IDX_DIM // LANES)]
    kc = zi[:, IDX_HEADS * IDX_DIM:IDX_HEADS * IDX_DIM + LANES]
    kiw = jnp.where(lax.broadcasted_iota(I32, kc.shape, 1) < IDX_DIM, rope(kc), kc)
    return dict(q=q, k=k, v=v, u=u, qic=qic, kiw=kiw)


def _conv_post(acc, lng, lnb):
    mu = jnp.mean(acc, axis=-1, keepdims=True)
    d = acc - mu
    var = jnp.mean(d * d, axis=-1, keepdims=True)
    y = d * lax.rsqrt(var + LN_EPS) * lng + lnb
    return y * _sigmoid(y)


def _inproj_prompt_kernel(x_ref, g_ref, wm_ref, wih_ref, wil_ref, rc_ref, ra_ref, rb_ref,
                          cw_ref, cb_ref, lng_ref, lnb_ref,
                          qt_ref, kt32_ref, vt32_ref, kb_ref, vto_ref, kit32_ref, kib_ref,
                          qit_ref, wit_ref, co_ref, ctail_ref, ubuf, ush):
    i = pl.program_id(1)
    tt = x_ref.shape[1]

    @pl.when(i == 0)
    def _():
        ubuf[0:HALO, :] = jnp.zeros((HALO, ubuf.shape[1]), F32)

    @pl.when(i > 0)
    def _():
        ubuf[0:HALO, :] = ubuf[tt:tt + HALO, :]

    def conv_branch(u):
        ubuf[HALO:HALO + tt, :] = u
        ctail_ref[0] = ubuf[tt:tt + HALO, :]

        span = ush.shape[1]
        for s in range(1, 8):
            ush[s - 1] = ubuf[pl.ds(s, span), :]

        rc_rows = 64
        cb = cb_ref[...]
        lng = lng_ref[...]
        lnb = lnb_ref[...]
        off = HALO - (CONV_WIDTH - 1)
        for r in range(tt // rc_rows):
            acc = jnp.broadcast_to(cb, (rc_rows, cb.shape[1]))
            for j in range(CONV_WIDTH):
                a, s = divmod(off + j, 8)
                rows = pl.ds(8 * a + r * rc_rows, rc_rows)
                win = ubuf[rows, :] if s == 0 else ush[s - 1, rows, :]
                acc = acc + win * cw_ref[j:j + 1, :]
            co_ref[0, r * rc_rows:(r + 1) * rc_rows, :] = _conv_post(acc, lng, lnb).astype(BF16)

    p = _inproj(x_ref[0], g_ref[...], wm_ref, wih_ref, wil_ref, rc_ref[...], ra_ref[...], rb_ref[...], conv_branch)
    for hh in range(ATTN_HEADS):
        qt_ref[0, hh] = p["q"][hh].T.astype(BF16)
    kt32_ref[0] = p["k"].T
    kb_ref[0] = p["k"].astype(BF16)
    vt = p["v"].T
    vt32_ref[0] = vt
    ones = jnp.ones((HEAD_DIM, tt), F32)
    for g in range(N_KV_HEADS):
        vto_ref[0, g] = jnp.concatenate([vt[g * HEAD_DIM:(g + 1) * HEAD_DIM], ones], axis=0).astype(BF16)
    kiw = p["kiw"]
    kib_ref[0] = kiw[:, :IDX_DIM].astype(BF16)
    kiwt = kiw.T
    kit32_ref[0] = kiwt[:IDX_DIM]
    wit_ref[0] = kiwt[IDX_DIM:IDX_DIM + IDX_HEADS]
    for c, chunk in enumerate(p["qic"]):
        ct = chunk.T.astype(BF16)
        for half in range(LANES // IDX_DIM):
            qit_ref[0, c * (LANES // IDX_DIM) + half] = ct[half * IDX_DIM:(half + 1) * IDX_DIM]


def _inproj_sample_kernel(x_ref, g_ref, wm_ref, wih_ref, wil_ref, rc_ref, ra_ref, rb_ref,
                          cw_ref, cb_ref, lng_ref, lnb_ref, st_ref,
                          qpad_ref, k32_ref, v32_ref, kb_ref, vb_ref, ki32_ref, kih_ref, kil_ref,
                          qih_ref, qil_ref, wi_ref, co_ref, cnew_ref, fbuf, uscr):
    hist, db, cch = st_ref.shape
    ds = x_ref.shape[0] // db

    def conv_branch(u):
        nch = cch // LANES
        for cc in range(nch):
            uscr[cc] = u[:, cc * LANES:(cc + 1) * LANES]
        fbuf[0:hist] = st_ref[...]
        for t in range(ds):
            for cc in range(nch):
                fbuf[hist + t, :, cc * LANES:(cc + 1) * LANES] = uscr[cc, pl.ds(t, db, stride=ds), :]
        acc = jnp.broadcast_to(cb_ref[...].reshape(1, 1, cch), (ds, db, cch))
        for j in range(CONV_WIDTH):
            acc = acc + fbuf[j:j + ds] * cw_ref[j:j + 1, :].reshape(1, 1, cch)
        y = _conv_post(acc.reshape(ds * db, cch), lng_ref[...], lnb_ref[...])
        for t in range(ds):
            for cc in range(nch):
                uscr[cc, pl.ds(t, db, stride=ds), :] = y[t * db:(t + 1) * db, cc * LANES:(cc + 1) * LANES]
        for cc in range(nch):
            co_ref[:, cc * LANES:(cc + 1) * LANES] = uscr[cc].astype(BF16)
        cnew_ref[...] = fbuf[ds:ds + hist]

    p = _inproj(x_ref[...], g_ref[...], wm_ref, wih_ref, wil_ref, rc_ref[...], ra_ref[...], rb_ref[...], conv_branch)
    for hh in range(ATTN_HEADS):
        qpad_ref[:, hh * LANES:(hh + 1) * LANES] = p["q"][hh].astype(BF16)
    k32_ref[...] = p["k"]
    v32_ref[...] = p["v"]
    kb_ref[...] = p["k"].astype(BF16)
    vb_ref[...] = p["v"].astype(BF16)
    ki = p["kiw"][:, :IDX_DIM]
    ki32_ref[...] = ki
    kih, kil = _split(ki)
    kih_ref[...] = kih
    kil_ref[...] = kil
    wi_ref[...] = p["kiw"][:, IDX_DIM:IDX_DIM + IDX_HEADS]
    per_chunk = LANES // IDX_DIM
    for hh in range(IDX_HEADS):
        half = hh % per_chunk
        hi, lo = _split(p["qic"][hh // per_chunk][:, half * IDX_DIM:(half + 1) * IDX_DIM])
        qih_ref[hh] = hi
        qil_ref[hh] = lo


def _inproj_prompt(x, g_pre, wm, wih, wil, rc, ra, rb, cw, cb, lng, lnb):
    b, t, d = x.shape
    tt = TOK_TILE
    cch = cw.shape[1]
    full = lambda a: pl.BlockSpec(a.shape, lambda bi, i: (0,) * a.ndim)
    tok = lambda w: pl.BlockSpec((1, tt, w), lambda bi, i: (bi, i, 0))
    feat = lambda w: pl.BlockSpec((1, w, tt), lambda bi, i: (bi, 0, i))
    tab = pl.BlockSpec((tt, LANES), lambda bi, i: (i, 0))
    hfeat = lambda n, w: pl.BlockSpec((1, n, w, tt), lambda bi, i: (bi, 0, 0, i))
    sds = jax.ShapeDtypeStruct
    out_shape = [sds((b, ATTN_HEADS, LANES, t), BF16), sds((b, LANES, t), F32), sds((b, LANES, t), F32),
                 sds((b, t, LANES), BF16), sds((b, N_KV_HEADS, LANES, t), BF16),
                 sds((b, IDX_DIM, t), F32), sds((b, t, IDX_DIM), BF16),
                 sds((b, IDX_HEADS, IDX_DIM, t), BF16),
                 sds((b, IDX_HEADS, t), F32), sds((b, t, cch), BF16), sds((b, HALO, cch), F32)]
    out_specs = [hfeat(ATTN_HEADS, LANES), feat(LANES), feat(LANES), tok(LANES), hfeat(N_KV_HEADS, LANES),
                 feat(IDX_DIM), tok(IDX_DIM), hfeat(IDX_HEADS, IDX_DIM), feat(IDX_HEADS), tok(cch),
                 pl.BlockSpec((1, HALO, cch), lambda bi, i: (bi, 0, 0))]
    return pl.pallas_call(
        _inproj_prompt_kernel,
        grid=(b, t // tt),
        in_specs=[tok(d), full(g_pre), full(wm), full(wih), full(wil), tab, tab, tab,
                  full(cw), full(cb), full(lng), full(lnb)],
        out_specs=out_specs,
        out_shape=out_shape,
        scratch_shapes=[pltpu.VMEM((tt + HALO, cch), F32), pltpu.VMEM((7, tt + HALO - 8, cch), F32)],
        compiler_params=pltpu.CompilerParams(dimension_semantics=("arbitrary", "arbitrary"),
                                             vmem_limit_bytes=VMEM_LIMIT),
        name="inproj_prompt",
    )(x, g_pre, wm, wih, wil, rc, ra, rb, cw, cb, lng, lnb)


def _inproj_sample(x, g_pre, wm, wih, wil, rc, ra, rb, cw, cb, lng, lnb, state):
    n, d = x.shape
    hist, db, cch = state.shape
    ds = n // db
    sds = jax.ShapeDtypeStruct
    out_shape = [sds((n, ATTN_HEADS * LANES), BF16), sds((n, LANES), F32), sds((n, LANES), F32),
                 sds((n, LANES), BF16), sds((n, LANES), BF16),
                 sds((n, IDX_DIM), F32), sds((n, IDX_DIM), BF16), sds((n, IDX_DIM), BF16),
                 sds((IDX_HEADS, n, IDX_DIM), BF16), sds((IDX_HEADS, n, IDX_DIM), BF16),
                 sds((n, IDX_HEADS), F32), sds((n, cch), BF16), sds((hist, db, cch), F32)]
    return pl.pallas_call(
        _inproj_sample_kernel,
        out_shape=out_shape,
        scratch_shapes=[pltpu.VMEM((hist + ds, db, cch), F32), pltpu.VMEM((cch // LANES, n, LANES), F32)],
        compiler_params=pltpu.CompilerParams(vmem_limit_bytes=VMEM_LIMIT),
        name="inproj_sample",
    )(x, g_pre, wm, wih, wil, rc, ra, rb, cw, cb, lng, lnb, state)


def _key_to_f32(key):
    bits = jnp.where(key < 0, key ^ INT_MAX, key)
    return lax.bitcast_convert_type(bits, F32)


def _row_reduce(sources, rows, fn, op, init):
    c = jnp.full((rows, LANES), init, F32)
    for get, nblk, width, col0 in sources:
        def body(j, c, get=get, width=width, col0=col0):
            blk = get(j)
            col = col0 + j * width + lax.broadcasted_iota(I32, blk.shape, 1)
            val = fn(blk, col)
            for cc in range(width // LANES):
                c = op(c, val[:, cc * LANES:(cc + 1) * LANES])
            return c
        c = lax.fori_loop(0, nblk, body, c)
    return c


def _fold_rows(x, op, part):
    groups = [x[r:r + part] for r in range(0, x.shape[0], part)]
    parts = groups[:FOLD_CHAINS]
    for k, g in enumerate(groups[FOLD_CHAINS:]):
        parts[k % FOLD_CHAINS] = op(parts[k % FOLD_CHAINS], g)
    while len(parts) > 1:
        parts = [op(parts[k], parts[k + 1]) if k + 1 < len(parts) else parts[k] for k in range(0, len(parts), 2)]
    return parts[0]


class _KeyMajor(NamedTuple):
    get: Callable
    nblk: Any
    q: int


def _vshape(sources, rows):
    return (1, sources.q) if isinstance(sources, _KeyMajor) else (rows, 1)


def _scan(sources, rows, fn, op, red, init):
    if not isinstance(sources, _KeyMajor):
        return red(_row_reduce(sources, rows, fn, op, init), axis=1, keepdims=True)
    def body(j, c):
        for blk, key0 in sources.get(j):
            key = key0 + lax.broadcasted_iota(I32, blk.shape, 0)
            c = op(c, _fold_rows(fn(blk, key), op, 8))
        return c

    c = lax.fori_loop(0, sources.nblk, body, jnp.full((8, sources.q), init, F32))
    return red(c, axis=0, keepdims=True)


def _count(sources, rows, pred):
    return _scan(sources, rows, lambda blk, col: jnp.where(pred(blk, col), 1.0, 0.0), jnp.add, jnp.sum, 0.0)


def _topk_select(sources, rows, topk, vmin, vmax, nvis, cpos, cnn):
    kf = float(topk)
    few = nvis <= kf
    zero_top = (cpos <= kf) & (cnn >= kf) & jnp.logical_not(few)
    up = cpos > kf

    def unsettled(carry):
        return (carry[0] < BISECT_STEPS) & (jnp.min(carry[6]) < 0.5)

    logit = lambda c: jnp.log((c + 0.5) / (nvis - c + 0.5))
    z_k = logit(kf)

    def step(carry):
        it, lo, hi, clo, chi, thr, done = carry
        z_lo, z_hi = logit(clo), logit(chi)
        den = z_lo - z_hi
        t = jnp.clip(jnp.where(den > 0.0, (z_lo - z_k) / den, 0.5), 1.0 / 32.0, 31.0 / 32.0)
        t = jnp.where((it & 3) == 3, 0.5, t)
        mid = lo + (hi - lo) * t
        c = _count(sources, rows, lambda blk, col: blk > mid)
        hit = (c == kf) & (done < 0.5)
        above, below = c > kf, c < kf
        return (it + 1, jnp.where(above, mid, lo), jnp.where(below, mid, hi),
                jnp.where(above, c, clo), jnp.where(below, c, chi),
                jnp.where(hit, mid, thr), jnp.where(hit, 1.0, done))

    carry0 = (jnp.int32(0), jnp.where(up, 0.0, vmin), jnp.where(up, vmax, 0.0),
              jnp.where(up, cpos, nvis), jnp.where(up, 0.0, cpos),
              jnp.where(few, -jnp.inf, 0.0), jnp.where(few | zero_top, 1.0, 0.0))
    thr, done = lax.while_loop(unsettled, step, carry0)[5:]
    settled = done > 0.5
    zero_need = kf - cpos
    cidx = jnp.where(zero_top & (zero_need > 0.0), INT_MAX, -1)
    need = jnp.where(zero_top, zero_need, 0.0)
    cut = jnp.where(zero_top & (cnn > kf) & (zero_need > 0.0), 1.0, 0.0)

    def exact(_):
        thr_e, cnt_e = _kth_largest_exact(sources, rows, topk)
        need_e = kf - _count(sources, rows, lambda blk, col: blk > thr_e)
        fin = thr_e > -jnp.inf
        return (jnp.where(settled, thr, thr_e), jnp.where(settled, cidx, jnp.where(fin, INT_MAX, -1)),
                jnp.where(settled, need, need_e), jnp.where(settled, cut, jnp.where(fin & (cnt_e > kf), 1.0, 0.0)))

    return lax.cond(jnp.min(done) < 0.5, exact, lambda _: (thr, cidx, need, cut), 0)


def _tie_cutoffs(slab_sources, nrows, ncols_max, thr_ref, need_ref, cut_ref, cidx_ref):
    n_it = int(np.ceil(np.log2(ncols_max + 1))) + 1

    def slab(sl, _):
        rs = pl.ds(pl.multiple_of(sl * 8, 8), 8)
        cut = cut_ref[rs, :] > 0.5

        @pl.when(jnp.max(cut_ref[rs, :]) > 0.5)
        def _():
            thr = thr_ref[rs, :]
            need = need_ref[rs, :]
            src = slab_sources(pl.multiple_of(sl * 8, 8))

            def bis_c(_, carry):
                lo_c, hi_c = carry
                mid = (lo_c + hi_c) >> 1
                ge = _count(src, 8, lambda blk, col: (blk == thr) & (col <= mid)) >= need
                return jnp.where(ge, lo_c, mid), jnp.where(ge, mid, hi_c)

            _, hi_c = lax.fori_loop(0, n_it, bis_c, (jnp.full((8, 1), -1, I32),
                                                     jnp.full((8, 1), ncols_max - 1, I32)))
            cidx_ref[rs, :] = jnp.where(cut, hi_c, cidx_ref[rs, :])
        return 0

    lax.fori_loop(0, nrows // 8, slab, 0)


def _drop_excess_ties(get_blk, put_blk, nblk, width, q, thr, need, cut):
    tri = jnp.where(lax.broadcasted_iota(I32, (width, width), 1) <= lax.broadcasted_iota(I32, (width, width), 0),
                    1.0, 0.0).astype(BF16)
    marked = cut > 0.5

    def body(j, before):
        blk = get_blk(j)
        tied = (blk == thr) & marked
        rank = _dot(tri, jnp.where(tied, 1.0, 0.0).astype(BF16))
        put_blk(j, jnp.where(tied & (before + rank > need), -jnp.inf, blk))
        return before + jnp.max(rank, axis=0, keepdims=True)

    lax.fori_loop(0, nblk, body, jnp.zeros((1, q), F32))


def _kth_largest_exact(sources, rows, topk):
    kf = float(topk)

    def bis(_, carry):
        lo, hi = carry
        mid = (lo >> 1) + (hi >> 1) + (lo & hi & 1)
        thr = _key_to_f32(mid)
        ge = _count(sources, rows, lambda blk, col: blk >= thr) >= kf
        return jnp.where(ge, mid, lo), jnp.where(ge, hi, mid)

    lo0 = jnp.full(_vshape(sources, rows), KEY_NEG_INF, I32)
    hi0 = jnp.full(_vshape(sources, rows), KEY_NAN_HI, I32)
    _, hi = lax.fori_loop(0, 32, bis, (lo0, hi0))

    def below(upper):
        m = _scan(sources, rows, lambda blk, col: jnp.where(blk < upper, blk, -jnp.inf),
                  jnp.maximum, jnp.max, -jnp.inf)
        return m, _count(sources, rows, lambda blk, col: blk >= m)

    def short(carry):
        return jnp.max(jnp.where(carry[2] < kf, 1.0, 0.0)) > 0.0

    def lower(carry):
        upper, m, c = carry
        upper = jnp.where(c < kf, m, upper)
        m, c = below(upper)
        return upper, m, c

    upper0 = _key_to_f32(hi)
    m0, c0 = below(upper0)
    _, thr, cnt = lax.while_loop(short, lower, (upper0, m0, c0))
    return thr, cnt


def _selected(blk, col, thr, cidx):
    return (blk > thr) | ((blk == thr) & (col <= cidx))


def _attn_prompt_kernel(qt_ref, qit_ref, wit_ref, kib_ref, kb_ref, vto_ref, o_ref,
                        sc_ref, vmx_ref, vmn_ref, cp_ref, cn_ref, thr_ref, cidx_ref, m_ref, acc_ref, *, topk):
    i = pl.program_id(1)
    tq = qt_ref.shape[3]
    tk = sc_ref.shape[1]
    nkb = (i * tq + tq + tk - 1) // tk
    q_pos = i * tq + lax.broadcasted_iota(I32, (tk, tq), 1)
    key_of = lambda j: j * tk + lax.broadcasted_iota(I32, (tk, tq), 0)
    wit = wit_ref[0]

    vmx_ref[...] = jnp.full(vmx_ref.shape, -jnp.inf, F32)
    vmn_ref[...] = jnp.full(vmn_ref.shape, jnp.inf, F32)
    cp_ref[...] = jnp.zeros(cp_ref.shape, F32)
    cn_ref[...] = jnp.zeros(cn_ref.shape, F32)

    def score_block(j, causal):
        kib = kib_ref[0, pl.ds(pl.multiple_of(j * tk, tk), tk), :]
        acc = jnp.zeros((tk, tq), F32)
        for hh in range(IDX_HEADS):
            acc = acc + jnp.maximum(_dot(kib, qit_ref[0, hh]), 0.0) * wit[hh:hh + 1, :]
        if causal:
            vis = key_of(j) <= q_pos
            sc = jnp.where(vis, acc, -jnp.inf)
            lowest = jnp.where(vis, acc, jnp.inf)
        else:
            sc = lowest = acc
        sc_ref[j] = sc
        vmx_ref[...] = jnp.maximum(vmx_ref[...], _fold_rows(sc, jnp.maximum, 8))
        vmn_ref[...] = jnp.minimum(vmn_ref[...], _fold_rows(lowest, jnp.minimum, 8))
        cp_ref[...] = cp_ref[...] + _fold_rows(jnp.where(sc > 0.0, 1.0, 0.0), jnp.add, 8)
        cn_ref[...] = cn_ref[...] + _fold_rows(jnp.where(sc >= 0.0, 1.0, 0.0), jnp.add, 8)
        return 0

    lax.fori_loop(0, nkb - 1, lambda j, _: score_block(j, False), 0)
    score_block(nkb - 1, True)

    @pl.when(nkb < sc_ref.shape[0])
    def _():
        sc_ref[nkb] = jnp.full((tk, tq), -jnp.inf, F32)

    pairs = _KeyMajor(lambda j: [(sc_ref[2 * j], 2 * j * tk), (sc_ref[2 * j + 1], (2 * j + 1) * tk)],
                      (nkb + 1) // 2, tq)
    nvis = (i * tq + 1 + lax.broadcasted_iota(I32, (1, tq), 1)).astype(F32)
    thr, cidx, need, cut = _topk_select(
        pairs, tq, topk, jnp.min(vmn_ref[...], axis=0, keepdims=True), jnp.max(vmx_ref[...], axis=0, keepdims=True),
        nvis, jnp.sum(cp_ref[...], axis=0, keepdims=True), jnp.sum(cn_ref[...], axis=0, keepdims=True))
    thr_ref[...] = thr
    cidx_ref[...] = cidx

    for half in range(tq // LANES):
        qs = slice(half * LANES, (half + 1) * LANES)

        @pl.when(jnp.max(cut[:, qs]) > 0.5)
        def _(qs=qs):
            def put(j, val):
                sc_ref[j, :, qs] = val
            _drop_excess_ties(lambda j: sc_ref[j, :, qs], put, nkb, tk, LANES, thr[:, qs], need[:, qs], cut[:, qs])

    m_ref[...] = jnp.full(m_ref.shape, NEG, F32)
    acc_ref[...] = jnp.zeros(acc_ref.shape, F32)
    thr = thr_ref[...]
    cidx = cidx_ref[...]

    def attn_block(j, _):
        ks = pl.ds(pl.multiple_of(j * tk, tk), tk)
        bias = jnp.where(_selected(sc_ref[j], key_of(j), thr, cidx), 0.0, NEG).astype(BF16)
        kb = kb_ref[0, ks, :]
        qk = lambda hh: _dot(kb, qt_ref[0, hh])
        ahead = [qk(hh) for hh in range(QK_AHEAD)]
        pending = []
        for hh in range(ATTN_HEADS + PV_DELAY):
            if hh < ATTN_HEADS:
                s = ahead.pop(0).astype(BF16) + bias
                if hh + QK_AHEAD < ATTN_HEADS:
                    ahead.append(qk(hh + QK_AHEAD))
                blk_max = jnp.max(_fold_rows(s, jnp.maximum, 16).astype(F32), axis=0, keepdims=True)
                m_old = m_ref[hh]
                m_new = jnp.maximum(m_old, blk_max)
                alpha = jnp.exp2(m_old - m_new)
                p = jnp.exp2(s - m_new.astype(BF16))
                m_ref[hh] = m_new
                pending.append((hh, alpha, p))
            if hh >= PV_DELAY:
                ph, palpha, pp = pending.pop(0)
                acc_ref[ph] = palpha * acc_ref[ph] + _dot(vto_ref[0, ph // KV_GROUP, :, ks], pp)
        return 0

    lax.fori_loop(0, nkb, attn_block, 0)

    outs = []
    for hh in range(ATTN_HEADS):
        acc = acc_ref[hh]
        outs.append((acc[:HEAD_DIM] / acc[HEAD_DIM:]).T)
    o_ref[0] = jnp.concatenate(outs, axis=1).astype(BF16)


def _attn_prompt(qt, qit, wit, kib, kb, vto, topk):
    b, _, _, t = qt.shape
    tq = Q_TILE
    tk = K_BLOCK
    assert t % tq == 0 and t % tk == 0 and tk % tq == 0
    hfeat = lambda n, w: pl.BlockSpec((1, n, w, tq), lambda bi, i: (bi, 0, 0, i))
    seq = lambda w: pl.BlockSpec((1, t, w), lambda bi, i: (bi, 0, 0))
    return pl.pallas_call(
        functools.partial(_attn_prompt_kernel, topk=topk),
        grid=(b, t // tq),
        in_specs=[hfeat(ATTN_HEADS, LANES), hfeat(IDX_HEADS, IDX_DIM),
                  pl.BlockSpec((1, IDX_HEADS, tq), lambda bi, i: (bi, 0, i)), seq(IDX_DIM), seq(LANES),
                  pl.BlockSpec((1, N_KV_HEADS, LANES, t), lambda bi, i: (bi, 0, 0, 0))],
        out_specs=pl.BlockSpec((1, tq, ATTN_WIDTH), lambda bi, i: (bi, i, 0)),
        out_shape=jax.ShapeDtypeStruct((b, t, ATTN_WIDTH), BF16),
        scratch_shapes=[pltpu.VMEM((t // tk + (t // tk) % 2, tk, tq), F32)] + [pltpu.VMEM((8, tq), F32)] * 4
                       + [pltpu.VMEM((1, tq), F32), pltpu.VMEM((1, tq), I32),
                          pltpu.VMEM((ATTN_HEADS, 1, tq), F32), pltpu.VMEM((ATTN_HEADS, LANES, tq), F32)],
        compiler_params=pltpu.CompilerParams(dimension_semantics=("arbitrary", "arbitrary"),
                                             vmem_limit_bytes=VMEM_LIMIT),
        name="attn_prompt",
    )(qt, qit, wit, kib, kb, vto)


def _score_sample_kernel(pt_ref, qih_ref, qil_ref, wcol_ref, knh_ref, knl_ref, *rest):
    pages = rest[:PAGES_PER_STEP]
    scp_ref, scn_ref, kh_s, kl_s = rest[PAGES_PER_STEP:]
    c = pl.program_id(1)
    ds = qih_ref.shape[1]
    page = pages[0].shape[1]
    qh = qih_ref[...].reshape(IDX_HEADS * ds, IDX_DIM)
    ql = qil_ref[...].reshape(IDX_HEADS * ds, IDX_DIM)
    wcol = wcol_ref[0]

    def combine(s):
        r = jnp.maximum(s, 0.0) * wcol
        return jnp.sum(r.reshape(IDX_HEADS, ds, r.shape[1]), axis=0)

    for pg in range(PAGES_PER_STEP):
        kh, kl = _split(pages[pg][...])
        kh_s[:, pg * page:(pg + 1) * page] = kh
        kl_s[:, pg * page:(pg + 1) * page] = kl
    kh = kh_s[...]
    scp_ref[0] = combine(_dot(qh, kh) + _dot(ql, kh) + _dot(qh, kl_s[...]))

    @pl.when(c == pl.num_programs(1) - 1)
    def _():
        knh = knh_ref[0]
        sn = combine(_dot_nt(qh, knh) + _dot_nt(ql, knh) + _dot_nt(qh, knl_ref[0]))
        tok = lax.broadcasted_iota(I32, sn.shape, 0)
        col = lax.broadcasted_iota(I32, sn.shape, 1)
        scn_ref[0] = jnp.where(col <= tok, sn, -jnp.inf)


def _select_sample_kernel(scp_ref, scn_ref, thr_ref, cidx_ref, need_ref, cut_ref, *, topk, blk_w):
    rows, past = scp_ref.shape

    def sources(rs):
        return [(lambda j: scp_ref[rs, pl.ds(pl.multiple_of(j * blk_w, blk_w), blk_w)], past // blk_w, blk_w, 0),
                (lambda j: scn_ref[rs, :], 1, scn_ref.shape[1], past)]

    src = sources(slice(None))
    lane_red = lambda fn, op, init, red: red(_row_reduce(src, rows, fn, op, init), axis=1, keepdims=True)
    vmax = lane_red(lambda blk, col: blk, jnp.maximum, -jnp.inf, jnp.max)
    vmin = lane_red(lambda blk, col: jnp.where(blk == -jnp.inf, jnp.inf, blk), jnp.minimum, jnp.inf, jnp.min)
    nvis = _count(src, rows, lambda blk, col: blk > -jnp.inf)
    cpos = _count(src, rows, lambda blk, col: blk > 0.0)
    cnn = _count(src, rows, lambda blk, col: blk >= 0.0)
    thr, cidx, need, cut = _topk_select(src, rows, topk, vmin, vmax, nvis, cpos, cnn)
    thr_ref[...] = thr
    cidx_ref[...] = cidx
    need_ref[...] = need
    cut_ref[...] = cut
    _tie_cutoffs(lambda r0: sources(pl.ds(r0, 8)), rows, past + scn_ref.shape[1],
                 thr_ref, need_ref, cut_ref, cidx_ref)


def _attn_sample_kernel(pt_ref, qpad_ref, scp_ref, scn_ref, thr_ref, cidx_ref, knew_ref, vnew_ref, *rest,
                        past_len):
    kpages = rest[:PAGES_PER_STEP]
    vpages = rest[PAGES_PER_STEP:2 * PAGES_PER_STEP]
    o_ref, kt_s, vt_s, m_ref, l_ref, acc_ref = rest[2 * PAGES_PER_STEP:]
    c = pl.program_id(1)
    ds = qpad_ref.shape[0]
    page = kpages[0].shape[1]
    thr = thr_ref[...]
    cidx = cidx_ref[...]

    @pl.when(c == 0)
    def _():
        m_ref[...] = jnp.full(m_ref.shape, NEG, F32)
        l_ref[...] = jnp.zeros(l_ref.shape, F32)
        acc_ref[...] = jnp.zeros(acc_ref.shape, F32)

    q_all = jnp.concatenate([qpad_ref[:, hh * LANES:(hh + 1) * LANES] for hh in range(ATTN_HEADS)], axis=0)

    def attend(sc, col, s_of_q, pv):
        bias1 = jnp.where(_selected(sc, col, thr, cidx), 0.0, NEG)
        s = s_of_q + jnp.concatenate([bias1] * ATTN_HEADS, axis=0)
        m_old = m_ref[...]
        m_new = jnp.maximum(m_old, jnp.max(s, axis=1, keepdims=True))
        alpha = jnp.exp2(m_old - m_new)
        p = jnp.exp2(s - m_new)
        l_ref[...] = alpha * l_ref[...] + jnp.sum(p, axis=1, keepdims=True)
        acc_ref[...] = alpha * acc_ref[...] + pv(p.astype(BF16))
        m_ref[...] = m_new

    for pg in range(PAGES_PER_STEP):
        kt_s[:, pg * page:(pg + 1) * page] = kpages[pg][...].astype(BF16)
        vt_s[:, pg * page:(pg + 1) * page] = vpages[pg][...].astype(BF16)
    width = PAGES_PER_STEP * page
    col = c * width + lax.broadcasted_iota(I32, (ds, width), 1)
    attend(scp_ref[0], col, _dot(q_all, kt_s[...]), lambda p: _dot_nt(p, vt_s[...]))

    @pl.when(c == pl.num_programs(1) - 1)
    def _():
        sc = scn_ref[0]
        ncol = past_len + lax.broadcasted_iota(I32, sc.shape, 1)
        attend(sc, ncol, _dot_nt(q_all, knew_ref[0]), lambda p: _dot(p, vnew_ref[0]))
        o = acc_ref[...] / l_ref[...]
        outs = []
        for hh in range(ATTN_HEADS):
            g = hh // KV_GROUP
            outs.append(o[hh * ds:(hh + 1) * ds, g * HEAD_DIM:(g + 1) * HEAD_DIM])
        o_ref[...] = jnp.concatenate(outs, axis=1).astype(BF16)


def _attn_sample(qpad, qih, qil, wi, kih, kil, kb, vb, pool_kt, pool_vt, pool_ikt, page_table, topk, db):
    n = qpad.shape[0]
    ds = n // db
    n_pool, _, page = pool_ikt.shape
    n_pages = page_table.shape[1]
    past_len = n_pages * page
    pps = PAGES_PER_STEP
    n_chunks = n_pages // pps
    assert n_pages % pps == 0 and ds <= LANES

    wcol = wi.reshape(db, ds, IDX_HEADS).transpose(0, 2, 1).reshape(db, IDX_HEADS * ds, 1)
    padk = lambda a: jnp.pad(a.reshape(db, ds, a.shape[1]), ((0, 0), (0, LANES - ds), (0, 0)))
    knh, knl, knew, vnew = padk(kih), padk(kil), padk(kb), padk(vb)

    def page_spec(w, pg):
        return pl.BlockSpec((None, w, page), lambda b, c, pt: (pt[b, c * pps + pg], 0, 0))

    hm = pl.BlockSpec((IDX_HEADS, ds, IDX_DIM), lambda b, c, pt: (0, b, 0))
    seq3 = lambda r, w: pl.BlockSpec((1, r, w), lambda b, c, pt: (b, 0, 0))
    chunk = pl.BlockSpec((1, ds, pps * page), lambda b, c, pt: (b, 0, c))
    scp, scn = pl.pallas_call(
        _score_sample_kernel,
        grid_spec=pltpu.PrefetchScalarGridSpec(
            num_scalar_prefetch=1, grid=(db, n_chunks),
            in_specs=[hm, hm, seq3(IDX_HEADS * ds, 1), seq3(LANES, IDX_DIM), seq3(LANES, IDX_DIM)]
                     + [page_spec(IDX_DIM, pg) for pg in range(pps)],
            out_specs=[chunk, seq3(ds, LANES)],
            scratch_shapes=[pltpu.VMEM((IDX_DIM, pps * page), BF16), pltpu.VMEM((IDX_DIM, pps * page), BF16)]),
        out_shape=[jax.ShapeDtypeStruct((db, ds, past_len), F32), jax.ShapeDtypeStruct((db, ds, LANES), F32)],
        compiler_params=pltpu.CompilerParams(dimension_semantics=("arbitrary", "arbitrary"),
                                             vmem_limit_bytes=VMEM_LIMIT),
        name="score_sample",
    )(page_table, qih, qil, wcol, knh, knl, *([pool_ikt] * pps))

    rows = min(SEL_ROWS, n)
    blk_w = 512
    thr, cidx = pl.pallas_call(
        functools.partial(_select_sample_kernel, topk=topk, blk_w=blk_w),
        grid=(n // rows,),
        in_specs=[pl.BlockSpec((rows, past_len), lambda r: (r, 0)), pl.BlockSpec((rows, LANES), lambda r: (r, 0))],
        out_specs=[pl.BlockSpec((rows, 1), lambda r: (r, 0)), pl.BlockSpec((rows, 1), lambda r: (r, 0))],
        out_shape=[jax.ShapeDtypeStruct((n, 1), F32), jax.ShapeDtypeStruct((n, 1), I32)],
        scratch_shapes=[pltpu.VMEM((rows, 1), F32), pltpu.VMEM((rows, 1), F32)],
        compiler_params=pltpu.CompilerParams(dimension_semantics=("arbitrary",), vmem_limit_bytes=VMEM_LIMIT),
        name="select_sample",
    )(scp.reshape(n, past_len), scn.reshape(n, LANES))

    tokrows = lambda w: pl.BlockSpec((ds, w), lambda b, c, pt: (b, 0))
    nrow = ATTN_HEADS * ds
    return pl.pallas_call(
        functools.partial(_attn_sample_kernel, past_len=past_len),
        grid_spec=pltpu.PrefetchScalarGridSpec(
            num_scalar_prefetch=1, grid=(db, n_chunks),
            in_specs=[tokrows(ATTN_HEADS * LANES), chunk, seq3(ds, LANES),
                      tokrows(1), tokrows(1), seq3(LANES, LANES), seq3(LANES, LANES)]
                     + [page_spec(LANES, pg) for pg in range(pps)] * 2,
            out_specs=tokrows(ATTN_WIDTH),
            scratch_shapes=[pltpu.VMEM((LANES, pps * page), BF16), pltpu.VMEM((LANES, pps * page), BF16),
                            pltpu.VMEM((nrow, 1), F32), pltpu.VMEM((nrow, 1), F32),
                            pltpu.VMEM((nrow, LANES), F32)]),
        out_shape=jax.ShapeDtypeStruct((n, ATTN_WIDTH), BF16),
        compiler_params=pltpu.CompilerParams(dimension_semantics=("arbitrary", "arbitrary"),
                                             vmem_limit_bytes=VMEM_LIMIT),
        name="attn_sample",
    )(page_table, qpad, scp, scn, thr, cidx, knew, vnew, *([pool_kt] * pps), *([pool_vt] * pps))


def _mix_mlp_kernel(a_ref, c_ref, x_ref, woa_ref, woc_ref, g1_ref, g2_ref, wup_ref, wdn_ref, g3_ref, y_ref):
    m = _dot(a_ref[...], woa_ref[...]) + _dot(c_ref[...], woc_ref[...])
    x1 = x_ref[...] + _rms(m) * g1_ref[...]
    h = (_rms(x1) * g2_ref[...]).astype(BF16)
    dff = wup_ref.shape[1]
    f = jnp.zeros(x1.shape, F32)
    for cc in range(dff // FF_CHUNK):
        cs = slice(cc * FF_CHUNK, (cc + 1) * FF_CHUNK)
        up = jnp.maximum(_dot(h, wup_ref[:, cs]), 0.0)
        f = f + _dot((up * up).astype(BF16), wdn_ref[cs, :])
    y_ref[...] = x1 + _rms(f) * g3_ref[...]


def _mix_mlp(a, c, x, woa, woc, g1, g2, wup, wdn, g3):
    n, d = x.shape
    tm = min(MLP_TILE, n)
    full = lambda w: pl.BlockSpec(w.shape, lambda i: (0,) * w.ndim)
    tok = lambda w: pl.BlockSpec((tm, w), lambda i: (i, 0))
    return pl.pallas_call(
        _mix_mlp_kernel,
        grid=(n // tm,),
        in_specs=[tok(a.shape[1]), tok(c.shape[1]), tok(d), full(woa), full(woc), full(g1), full(g2),
                  full(wup), full(wdn), full(g3)],
        out_specs=tok(d),
        out_shape=jax.ShapeDtypeStruct((n, d), F32),
        compiler_params=pltpu.CompilerParams(dimension_semantics=("arbitrary",), vmem_limit_bytes=VMEM_LIMIT),
        name="mix_mlp",
    )(a, c, x, woa, woc, g1, g2, wup, wdn, g3)


def _rope_tables(pos):
    rot = HEAD_DIM // 4
    half = rot // 2
    inv_freq = jnp.power(ROPE_THETA, -jnp.arange(half, dtype=F32) * (2.0 / rot))
    dim = np.arange(LANES) % HEAD_DIM
    ang = pos.astype(F32)[:, None] * inv_freq[dim % half][None, :]
    cos, sin = jnp.cos(ang), jnp.sin(ang)
    rc = jnp.where((dim < rot)[None, :], cos, 1.0)
    ra = jnp.where(((dim >= half) & (dim < rot))[None, :], sin, 0.0)
    rb = jnp.where((dim < half)[None, :], -sin, 0.0)
    return rc, ra, rb


def _prep_w_in(w):
    d = w.shape[0]
    sizes = (ATTN_WIDTH, N_KV_HEADS * HEAD_DIM, N_KV_HEADS * HEAD_DIM, IDX_HEADS * IDX_DIM, IDX_DIM, IDX_HEADS)
    offs = np.cumsum((0,) + sizes)
    wq, wk, wv, wqi, wki, wwi = [w[:, offs[i]:offs[i + 1]] for i in range(6)]
    wu = w[:, offs[6]:]
    zpad = jnp.zeros((d, HEAD_DIM), w.dtype)
    qcols = []
    for hh in range(ATTN_HEADS):
        piece = wq[:, hh * HEAD_DIM:(hh + 1) * HEAD_DIM]
        qcols += [piece, zpad] if hh // KV_GROUP == 0 else [zpad, piece]
    wm = jnp.concatenate(qcols + [wk, wv, wu], axis=1).astype(BF16)
    widx = jnp.concatenate([wqi, wki, wwi, jnp.zeros((d, LANES - IDX_DIM - IDX_HEADS), w.dtype)], axis=1)
    wih, wil = _split(widx)
    return wm, wih, wil


def kernel(x_prompt, x_sample, cache_k, cache_v, cache_idx_k, state_conv, page_table, norm_mix_pre, w_in,
           conv_dw_w, conv_dw_b, conv_ln_g, conv_ln_b, w_out, norm_mix_post, norm_mlp_pre, w_up, w_down,
           norm_mlp_post):
    depth = w_in.shape[0]
    b, t, d = x_prompt.shape
    db, ds, _ = x_sample.shape
    n_pool, page = cache_k.shape[1], cache_k.shape[2]
    past_len = page_table.shape[1] * page
    topk_p = min(INDEX_TOPK, t // 4)
    topk_s = min(INDEX_TOPK, (past_len + ds) // 4)
    hist = CONV_WIDTH - 1

    rcp, rap, rbp = _rope_tables(jnp.arange(t, dtype=I32))
    rcs, ras, rbs = [jnp.tile(a, (db, 1)) for a in _rope_tables(past_len + jnp.arange(ds, dtype=I32))]

    xp = x_prompt
    xs = x_sample.reshape(db * ds, d)
    outs = [[] for _ in range(8)]
    row = lambda a: a.reshape(1, -1)
    for l in range(depth):
        wm, wih, wil = _prep_w_in(w_in[l])
        conv_args = (conv_dw_w[l], row(conv_dw_b[l]), row(conv_ln_g[l]), row(conv_ln_b[l]))
        woa = w_out[l, :ATTN_WIDTH].astype(BF16)
        woc = w_out[l, ATTN_WIDTH:].astype(BF16)
        mlp_args = (woa, woc, row(norm_mix_post[l]), row(norm_mlp_pre[l]), w_up[l].astype(BF16),
                    w_down[l].astype(BF16), row(norm_mlp_post[l]))

        (qt, kt32, vt32, kb, vto, kit32, kib, qit, wit, co, ctail) = _inproj_prompt(
            xp, row(norm_mix_pre[l]), wm, wih, wil, rcp, rap, rbp, *conv_args)
        ao = _attn_prompt(qt, qit, wit, kib, kb, vto, topk_p)
        xp = _mix_mlp(ao.reshape(b * t, -1), co.reshape(b * t, -1), xp.reshape(b * t, d), *mlp_args).reshape(b, t, d)
        unt = lambda a: a.reshape(b, N_KV_HEADS, HEAD_DIM, t).transpose(0, 3, 1, 2)
        outs[0].append(unt(kt32))
        outs[1].append(unt(vt32))
        outs[2].append(kit32.transpose(0, 2, 1))
        outs[3].append(ctail[:, HALO - hist:])

        pool_kt = cache_k[l].transpose(0, 2, 3, 1).reshape(n_pool, N_KV_HEADS * HEAD_DIM, page)
        pool_vt = cache_v[l].transpose(0, 2, 3, 1).reshape(n_pool, N_KV_HEADS * HEAD_DIM, page)
        pool_ikt = cache_idx_k[l].transpose(0, 2, 1)
        (qpad, k32, v32, kb, vb, ki32, kih, kil, qih, qil, wi, co, cnew) = _inproj_sample(
            xs, row(norm_mix_pre[l]), wm, wih, wil, rcs, ras, rbs, *conv_args, state_conv[l].transpose(1, 0, 2))
        ao = _attn_sample(qpad, qih, qil, wi, kih, kil, kb, vb, pool_kt, pool_vt, pool_ikt,
                          page_table, topk_s, db)
        xs = _mix_mlp(ao, co, xs, *mlp_args)
        outs[4].append(k32.reshape(db, ds, N_KV_HEADS, HEAD_DIM))
        outs[5].append(v32.reshape(db, ds, N_KV_HEADS, HEAD_DIM))
        outs[6].append(ki32.reshape(db, ds, IDX_DIM))
        outs[7].append(cnew.transpose(1, 0, 2))

    st = [jnp.stack(o, 0) for o in outs]
    return (xp, xs.reshape(db, ds, d), st[0], st[1], st[2], st[3], st[4], st[5], st[6], st[7])
```

```python
import functools
from typing import Any, Callable, NamedTuple

import numpy as np
import jax
import jax.numpy as jnp
from jax import lax
from jax.experimental import pallas as pl
from jax.experimental.pallas import tpu as pltpu

F32 = jnp.float32
BF16 = jnp.bfloat16
I32 = jnp.int32

ATTN_HEADS = 8
HEAD_DIM = 64
ATTN_WIDTH = ATTN_HEADS * HEAD_DIM
N_KV_HEADS = 2
KV_GROUP = ATTN_HEADS // N_KV_HEADS
IDX_HEADS = 8
IDX_DIM = 64
INDEX_TOPK = 256
ROPE_THETA = 500000.0
ROT_HALF = HEAD_DIM // 4 // 2
CONV_WIDTH = 31
RMS_EPS = 1e-6
LN_EPS = 1e-5
SCALE = 0.125
assert HEAD_DIM ** -0.5 == SCALE and IDX_DIM ** -0.5 == SCALE
assert N_KV_HEADS * HEAD_DIM == 128

LANES = 128
SUBLANES = 8
LOG2E = 1.4426950408889634
NEG = -(2.0 ** 100)
HALO = 32
TOK_TILE = 256
Q_TILE = 256
K_BLOCK = 512
SEL_ROWS = 128
PV_DELAY = 2
QK_AHEAD = 3
FOLD_CHAINS = 4
SCAN_CHAINS = 2
BISECT_STEPS = 28
MLP_TILE = 256
FF_CHUNK = 1024
PAGES_PER_STEP = 64
VMEM_LIMIT = 56 * 1024 * 1024

KEY_NEG_INF = -2139095041
KEY_NAN_HI = 2139095041
INT_MAX = 2147483647


def _dot(a, b):
    return jnp.dot(a, b, preferred_element_type=F32)


def _dot_nt(a, b):
    return lax.dot_general(a, b, (((1,), (1,)), ((), ())), preferred_element_type=F32)


def _split(x):
    hi = x.astype(BF16)
    lo = (x - hi.astype(F32)).astype(BF16)
    return hi, lo


def _rms(x):
    return x * lax.rsqrt(jnp.mean(x * x, axis=-1, keepdims=True) + RMS_EPS)


def _sigmoid(x):
    return 1.0 / (1.0 + jnp.exp(-x))


def _inproj(x, g_pre, wm_ref, wih_ref, wil_ref, rc, ra, rb, on_glu):
    h = _rms(x) * g_pre
    h_hi, h_lo = _split(h)
    nq = ATTN_HEADS * LANES
    nqkv = nq + 2 * LANES
    cch = wm_ref.shape[1] - nqkv
    zu = _dot(h_hi, wm_ref[:, nqkv:])
    u = zu[:, :cch // 2] * _sigmoid(zu[:, cch // 2:])
    on_glu(u)
    zm = _dot(h_hi, wm_ref[:, :nqkv])
    wih = wih_ref[...]
    zi = _dot(h_hi, wih) + _dot(h_lo, wih) + _dot(h_hi, wil_ref[...])

    def rope(c):
        return c * rc + pltpu.roll(c, ROT_HALF, 1) * ra + pltpu.roll(c, LANES - ROT_HALF, 1) * rb

    q = [rope(zm[:, c * LANES:(c + 1) * LANES]) * (SCALE * LOG2E) for c in range(ATTN_HEADS)]
    k = rope(zm[:, nq:nq + LANES])
    v = zm[:, nq + LANES:nq + 2 * LANES]
    qic = [rope(zi[:, c * LANES:(c + 1) * LANES]) * SCALE for c in range(IDX_HEADS * IDX_DIM // LANES)]
    kc = zi[:, IDX_HEADS * IDX_DIM:IDX_HEADS * IDX_DIM + LANES]
    kiw = jnp.where(lax.broadcasted_iota(I32, kc.shape, 1) < IDX_DIM, rope(kc), kc)
    return dict(q=q, k=k, v=v, u=u, qic=qic, kiw=kiw)


def _conv_post(acc, lng, lnb):
    mu = jnp.mean(acc, axis=-1, keepdims=True)
    d = acc - mu
    var = jnp.mean(d * d, axis=-1, keepdims=True)
    y = d * lax.rsqrt(var + LN_EPS) * lng + lnb
    return y * _sigmoid(y)


def _inproj_prompt_kernel(x_ref, g_ref, wm_ref, wih_ref, wil_ref, rc_ref, ra_ref, rb_ref,
                          cw_ref, cb_ref, lng_ref, lnb_ref,
                          qt_ref, kt32_ref, vt32_ref, kb_ref, vto_ref, kit32_ref, kib_ref,
                          qit_ref, wit_ref, co_ref, ctail_ref, ubuf, ush):
    i = pl.program_id(1)
    tt = x_ref.shape[1]

    @pl.when(i == 0)
    def _():
        ubuf[0:HALO, :] = jnp.zeros((HALO, ubuf.shape[1]), F32)

    @pl.when(i > 0)
    def _():
        ubuf[0:HALO, :] = ubuf[tt:tt + HALO, :]

    def conv_branch(u):
        ubuf[HALO:HALO + tt, :] = u
        ctail_ref[0] = ubuf[tt:tt + HALO, :]

        span = ush.shape[1]
        for s in range(1, SUBLANES):
            ush[s - 1] = ubuf[pl.ds(s, span), :]

        rc_rows = 64
        cb = cb_ref[...]
        lng = lng_ref[...]
        lnb = lnb_ref[...]
        off = HALO - (CONV_WIDTH - 1)
        for r in range(tt // rc_rows):
            acc = jnp.broadcast_to(cb, (rc_rows, cb.shape[1]))
            for j in range(CONV_WIDTH):
                a, s = divmod(off + j, SUBLANES)
                rows = pl.ds(SUBLANES * a + r * rc_rows, rc_rows)
                win = ubuf[rows, :] if s == 0 else ush[s - 1, rows, :]
                acc = acc + win * cw_ref[j:j + 1, :]
            co_ref[0, r * rc_rows:(r + 1) * rc_rows, :] = _conv_post(acc, lng, lnb).astype(BF16)

    p = _inproj(x_ref[0], g_ref[...], wm_ref, wih_ref, wil_ref, rc_ref[...], ra_ref[...], rb_ref[...], conv_branch)
    for hh in range(ATTN_HEADS):
        qt_ref[0, hh] = p["q"][hh].T.astype(BF16)
    kt32_ref[0] = p["k"].T
    kb_ref[0] = p["k"].astype(BF16)
    vt = p["v"].T
    vt32_ref[0] = vt
    ones = jnp.ones((HEAD_DIM, tt), F32)
    for g in range(N_KV_HEADS):
        vto_ref[0, g] = jnp.concatenate([vt[g * HEAD_DIM:(g + 1) * HEAD_DIM], ones], axis=0).astype(BF16)
    kiw = p["kiw"]
    kib_ref[0] = kiw[:, :IDX_DIM].astype(BF16)
    kiwt = kiw.T
    kit32_ref[0] = kiwt[:IDX_DIM]
    wit_ref[0] = kiwt[IDX_DIM:IDX_DIM + IDX_HEADS]
    for c, chunk in enumerate(p["qic"]):
        ct = chunk.T.astype(BF16)
        for half in range(LANES // IDX_DIM):
            qit_ref[0, c * (LANES // IDX_DIM) + half] = ct[half * IDX_DIM:(half + 1) * IDX_DIM]


def _inproj_sample_kernel(x_ref, g_ref, wm_ref, wih_ref, wil_ref, rc_ref, ra_ref, rb_ref,
                          cw_ref, cb_ref, lng_ref, lnb_ref, st_ref,
                          qpad_ref, k32_ref, v32_ref, kb_ref, vb_ref, ki32_ref, kih_ref, kil_ref,
                          qih_ref, qil_ref, wi_ref, co_ref, cnew_ref, fbuf, uscr):
    hist, db, cch = st_ref.shape
    ds = x_ref.shape[0] // db

    def conv_branch(u):
        nch = cch // LANES
        for cc in range(nch):
            uscr[cc] = u[:, cc * LANES:(cc + 1) * LANES]
        fbuf[0:hist] = st_ref[...]
        for t in range(ds):
            for cc in range(nch):
                fbuf[hist + t, :, cc * LANES:(cc + 1) * LANES] = uscr[cc, pl.ds(t, db, stride=ds), :]
        acc = jnp.broadcast_to(cb_ref[...].reshape(1, 1, cch), (ds, db, cch))
        for j in range(CONV_WIDTH):
            acc = acc + fbuf[j:j + ds] * cw_ref[j:j + 1, :].reshape(1, 1, cch)
        y = _conv_post(acc.reshape(ds * db, cch), lng_ref[...], lnb_ref[...])
        for t in range(ds):
            for cc in range(nch):
                uscr[cc, pl.ds(t, db, stride=ds), :] = y[t * db:(t + 1) * db, cc * LANES:(cc + 1) * LANES]
        for cc in range(nch):
            co_ref[:, cc * LANES:(cc + 1) * LANES] = uscr[cc].astype(BF16)
        cnew_ref[...] = fbuf[ds:ds + hist]

    p = _inproj(x_ref[...], g_ref[...], wm_ref, wih_ref, wil_ref, rc_ref[...], ra_ref[...], rb_ref[...], conv_branch)
    for hh in range(ATTN_HEADS):
        qpad_ref[:, hh * LANES:(hh + 1) * LANES] = p["q"][hh].astype(BF16)
    k32_ref[...] = p["k"]
    v32_ref[...] = p["v"]
    kb_ref[...] = p["k"].astype(BF16)
    vb_ref[...] = p["v"].astype(BF16)
    ki = p["kiw"][:, :IDX_DIM]
    ki32_ref[...] = ki
    kih, kil = _split(ki)
    kih_ref[...] = kih
    kil_ref[...] = kil
    wi_ref[...] = p["kiw"][:, IDX_DIM:IDX_DIM + IDX_HEADS]
    per_chunk = LANES // IDX_DIM
    for hh in range(IDX_HEADS):
        half = hh % per_chunk
        hi, lo = _split(p["qic"][hh // per_chunk][:, half * IDX_DIM:(half + 1) * IDX_DIM])
        qih_ref[hh] = hi
        qil_ref[hh] = lo


def _inproj_prompt(x, g_pre, wm, wih, wil, rc, ra, rb, cw, cb, lng, lnb):
    b, t, d = x.shape
    tt = TOK_TILE
    cch = cw.shape[1]
    full = lambda a: pl.BlockSpec(a.shape, lambda bi, i: (0,) * a.ndim)
    tok = lambda w: pl.BlockSpec((1, tt, w), lambda bi, i: (bi, i, 0))
    feat = lambda w: pl.BlockSpec((1, w, tt), lambda bi, i: (bi, 0, i))
    tab = pl.BlockSpec((tt, LANES), lambda bi, i: (i, 0))
    hfeat = lambda n, w: pl.BlockSpec((1, n, w, tt), lambda bi, i: (bi, 0, 0, i))
    sds = jax.ShapeDtypeStruct
    out_shape = [sds((b, ATTN_HEADS, LANES, t), BF16), sds((b, LANES, t), F32), sds((b, LANES, t), F32),
                 sds((b, t, LANES), BF16), sds((b, N_KV_HEADS, LANES, t), BF16),
                 sds((b, IDX_DIM, t), F32), sds((b, t, IDX_DIM), BF16),
                 sds((b, IDX_HEADS, IDX_DIM, t), BF16),
                 sds((b, IDX_HEADS, t), F32), sds((b, t, cch), BF16), sds((b, HALO, cch), F32)]
    out_specs = [hfeat(ATTN_HEADS, LANES), feat(LANES), feat(LANES), tok(LANES), hfeat(N_KV_HEADS, LANES),
                 feat(IDX_DIM), tok(IDX_DIM), hfeat(IDX_HEADS, IDX_DIM), feat(IDX_HEADS), tok(cch),
                 pl.BlockSpec((1, HALO, cch), lambda bi, i: (bi, 0, 0))]
    return pl.pallas_call(
        _inproj_prompt_kernel,
        grid=(b, t // tt),
        in_specs=[tok(d), full(g_pre), full(wm), full(wih), full(wil), tab, tab, tab,
                  full(cw), full(cb), full(lng), full(lnb)],
        out_specs=out_specs,
        out_shape=out_shape,
        scratch_shapes=[pltpu.VMEM((tt + HALO, cch), F32), pltpu.VMEM((SUBLANES - 1, tt + HALO - SUBLANES, cch), F32)],
        compiler_params=pltpu.CompilerParams(dimension_semantics=("arbitrary", "arbitrary"),
                                             vmem_limit_bytes=VMEM_LIMIT),
        name="inproj_prompt",
    )(x, g_pre, wm, wih, wil, rc, ra, rb, cw, cb, lng, lnb)


def _inproj_sample(x, g_pre, wm, wih, wil, rc, ra, rb, cw, cb, lng, lnb, state):
    n, d = x.shape
    hist, db, cch = state.shape
    ds = n // db
    sds = jax.ShapeDtypeStruct
    out_shape = [sds((n, ATTN_HEADS * LANES), BF16), sds((n, LANES), F32), sds((n, LANES), F32),
                 sds((n, LANES), BF16), sds((n, LANES), BF16),
                 sds((n, IDX_DIM), F32), sds((n, IDX_DIM), BF16), sds((n, IDX_DIM), BF16),
                 sds((IDX_HEADS, n, IDX_DIM), BF16), sds((IDX_HEADS, n, IDX_DIM), BF16),
                 sds((n, IDX_HEADS), F32), sds((n, cch), BF16), sds((hist, db, cch), F32)]
    return pl.pallas_call(
        _inproj_sample_kernel,
        out_shape=out_shape,
        scratch_shapes=[pltpu.VMEM((hist + ds, db, cch), F32), pltpu.VMEM((cch // LANES, n, LANES), F32)],
        compiler_params=pltpu.CompilerParams(vmem_limit_bytes=VMEM_LIMIT),
        name="inproj_sample",
    )(x, g_pre, wm, wih, wil, rc, ra, rb, cw, cb, lng, lnb, state)


def _key_to_f32(key):
    bits = jnp.where(key < 0, key ^ INT_MAX, key)
    return lax.bitcast_convert_type(bits, F32)


def _row_reduce(sources, rows, fn, op, init):
    c = jnp.full((rows, LANES), init, F32)
    for get, nblk, width, col0 in sources:
        def body(j, c, get=get, width=width, col0=col0):
            blk = get(j)
            col = col0 + j * width + lax.broadcasted_iota(I32, blk.shape, 1)
            val = fn(blk, col)
            for cc in range(width // LANES):
                c = op(c, val[:, cc * LANES:(cc + 1) * LANES])
            return c
        c = lax.fori_loop(0, nblk, body, c)
    return c


def _fold_rows(x, op, part, chains=FOLD_CHAINS):
    groups = [x[r:r + part] for r in range(0, x.shape[0], part)]
    parts = groups[:chains]
    for k, g in enumerate(groups[chains:]):
        parts[k % chains] = op(parts[k % chains], g)
    while len(parts) > 1:
        parts = [op(parts[k], parts[k + 1]) if k + 1 < len(parts) else parts[k] for k in range(0, len(parts), 2)]
    return parts[0]


class _KeyMajor(NamedTuple):
    get: Callable
    nblk: Any
    q: int


def _vshape(sources, rows):
    return (1, sources.q) if isinstance(sources, _KeyMajor) else (rows, 1)


def _scan(sources, rows, fn, op, red, init):
    if not isinstance(sources, _KeyMajor):
        return red(_row_reduce(sources, rows, fn, op, init), axis=1, keepdims=True)
    def body(j, c):
        for blk, key0 in sources.get(j):
            key = key0 + lax.broadcasted_iota(I32, blk.shape, 0)
            c = op(c, _fold_rows(fn(blk, key), op, SUBLANES, SCAN_CHAINS))
        return c

    c = lax.fori_loop(0, sources.nblk, body, jnp.full((SUBLANES, sources.q), init, F32))
    return red(c, axis=0, keepdims=True)


def _count(sources, rows, pred):
    return _scan(sources, rows, lambda blk, col: jnp.where(pred(blk, col), 1.0, 0.0), jnp.add, jnp.sum, 0.0)


def _topk_select(sources, rows, topk, vmin, vmax, nvis, cpos, cnn):
    kf = float(topk)
    few = nvis <= kf
    zero_top = (cpos <= kf) & (cnn >= kf) & jnp.logical_not(few)
    up = cpos > kf

    def unsettled(carry):
        return (carry[0] < BISECT_STEPS) & (jnp.min(carry[6]) < 0.5)

    logit = lambda c: jnp.log((c + 0.5) / (nvis - c + 0.5))
    z_k = logit(kf)

    def step(carry):
        it, lo, hi, clo, chi, thr, done = carry
        z_lo, z_hi = logit(clo), logit(chi)
        den = z_lo - z_hi
        t = jnp.clip(jnp.where(den > 0.0, (z_lo - z_k) / den, 0.5), 1.0 / 32.0, 31.0 / 32.0)
        t = jnp.where((it & 3) == 3, 0.5, t)
        mid = lo + (hi - lo) * t
        c = _count(sources, rows, lambda blk, col: blk > mid)
        hit = (c == kf) & (done < 0.5)
        above, below = c > kf, c < kf
        return (it + 1, jnp.where(above, mid, lo), jnp.where(below, mid, hi),
                jnp.where(above, c, clo), jnp.where(below, c, chi),
                jnp.where(hit, mid, thr), jnp.where(hit, 1.0, done))

    carry0 = (jnp.int32(0), jnp.where(up, 0.0, vmin), jnp.where(up, vmax, 0.0),
              jnp.where(up, cpos, nvis), jnp.where(up, 0.0, cpos),
              jnp.where(few, -jnp.inf, 0.0), jnp.where(few | zero_top, 1.0, 0.0))
    thr, done = lax.while_loop(unsettled, step, carry0)[5:]
    settled = done > 0.5
    zero_need = kf - cpos
    cidx = jnp.where(zero_top & (zero_need > 0.0), INT_MAX, -1)
    need = jnp.where(zero_top, zero_need, 0.0)
    cut = jnp.where(zero_top & (cnn > kf) & (zero_need > 0.0), 1.0, 0.0)

    def exact(_):
        thr_e, cnt_e = _kth_largest_exact(sources, rows, topk)
        need_e = kf - _count(sources, rows, lambda blk, col: blk > thr_e)
        fin = thr_e > -jnp.inf
        return (jnp.where(settled, thr, thr_e), jnp.where(settled, cidx, jnp.where(fin, INT_MAX, -1)),
                jnp.where(settled, need, need_e), jnp.where(settled, cut, jnp.where(fin & (cnt_e > kf), 1.0, 0.0)))

    return lax.cond(jnp.min(done) < 0.5, exact, lambda _: (thr, cidx, need, cut), 0)


def _tie_cutoffs(slab_sources, nrows, ncols_max, thr_ref, need_ref, cut_ref, cidx_ref):
    n_it = int(np.ceil(np.log2(ncols_max + 1))) + 1
    sr = SUBLANES

    def slab(sl, _):
        r0 = pl.multiple_of(sl * sr, sr)
        rs = pl.ds(r0, sr)
        cut = cut_ref[rs, :] > 0.5

        @pl.when(jnp.max(cut_ref[rs, :]) > 0.5)
        def _():
            thr = thr_ref[rs, :]
            need = need_ref[rs, :]
            src = slab_sources(r0)

            def bis_c(_, carry):
                lo_c, hi_c = carry
                mid = (lo_c + hi_c) >> 1
                ge = _count(src, sr, lambda blk, col: (blk == thr) & (col <= mid)) >= need
                return jnp.where(ge, lo_c, mid), jnp.where(ge, mid, hi_c)

            _, hi_c = lax.fori_loop(0, n_it, bis_c, (jnp.full((sr, 1), -1, I32),
                                                     jnp.full((sr, 1), ncols_max - 1, I32)))
            cidx_ref[rs, :] = jnp.where(cut, hi_c, cidx_ref[rs, :])
        return 0

    lax.fori_loop(0, nrows // sr, slab, 0)


def _drop_excess_ties(get_blk, put_blk, nblk, width, q, thr, need, cut):
    tri = jnp.where(lax.broadcasted_iota(I32, (width, width), 1) <= lax.broadcasted_iota(I32, (width, width), 0),
                    1.0, 0.0).astype(BF16)
    marked = cut > 0.5

    def body(j, before):
        blk = get_blk(j)
        tied = (blk == thr) & marked
        rank = _dot(tri, jnp.where(tied, 1.0, 0.0).astype(BF16))
        put_blk(j, jnp.where(tied & (before + rank > need), -jnp.inf, blk))
        return before + jnp.max(rank, axis=0, keepdims=True)

    lax.fori_loop(0, nblk, body, jnp.zeros((1, q), F32))


def _kth_largest_exact(sources, rows, topk):
    kf = float(topk)

    def bis(_, carry):
        lo, hi = carry
        mid = (lo >> 1) + (hi >> 1) + (lo & hi & 1)
        thr = _key_to_f32(mid)
        ge = _count(sources, rows, lambda blk, col: blk >= thr) >= kf
        return jnp.where(ge, mid, lo), jnp.where(ge, hi, mid)

    lo0 = jnp.full(_vshape(sources, rows), KEY_NEG_INF, I32)
    hi0 = jnp.full(_vshape(sources, rows), KEY_NAN_HI, I32)
    _, hi = lax.fori_loop(0, 32, bis, (lo0, hi0))

    def below(upper):
        m = _scan(sources, rows, lambda blk, col: jnp.where(blk < upper, blk, -jnp.inf),
                  jnp.maximum, jnp.max, -jnp.inf)
        return m, _count(sources, rows, lambda blk, col: blk >= m)

    def short(carry):
        return jnp.max(jnp.where(carry[2] < kf, 1.0, 0.0)) > 0.0

    def lower(carry):
        upper, m, c = carry
        upper = jnp.where(c < kf, m, upper)
        m, c = below(upper)
        return upper, m, c

    upper0 = _key_to_f32(hi)
    m0, c0 = below(upper0)
    _, thr, cnt = lax.while_loop(short, lower, (upper0, m0, c0))
    return thr, cnt


def _selected(blk, col, thr, cidx):
    return (blk > thr) | ((blk == thr) & (col <= cidx))


def _attn_prompt_kernel(qt_ref, qit_ref, wit_ref, kib_ref, kb_ref, vto_ref, o_ref,
                        sc_ref, vmx_ref, vmn_ref, cp_ref, cn_ref, thr_ref, cidx_ref, m_ref, acc_ref, *, topk):
    i = pl.program_id(1)
    tq = qt_ref.shape[3]
    tk = sc_ref.shape[1]
    nkb = (i * tq + tq + tk - 1) // tk
    q_pos = i * tq + lax.broadcasted_iota(I32, (tk, tq), 1)
    key_of = lambda j: j * tk + lax.broadcasted_iota(I32, (tk, tq), 0)
    wit = wit_ref[0]

    vmx_ref[...] = jnp.full(vmx_ref.shape, -jnp.inf, F32)
    vmn_ref[...] = jnp.full(vmn_ref.shape, jnp.inf, F32)
    cp_ref[...] = jnp.zeros(cp_ref.shape, F32)
    cn_ref[...] = jnp.zeros(cn_ref.shape, F32)

    def score_block(j, causal):
        kib = kib_ref[0, pl.ds(pl.multiple_of(j * tk, tk), tk), :]
        acc = jnp.zeros((tk, tq), F32)
        for hh in range(IDX_HEADS):
            acc = acc + jnp.maximum(_dot(kib, qit_ref[0, hh]), 0.0) * wit[hh:hh + 1, :]
        if causal:
            vis = key_of(j) <= q_pos
            sc = jnp.where(vis, acc, -jnp.inf)
            lowest = jnp.where(vis, acc, jnp.inf)
        else:
            sc = lowest = acc
        sc_ref[j] = sc
        vmx_ref[...] = jnp.maximum(vmx_ref[...], _fold_rows(sc, jnp.maximum, SUBLANES))
        vmn_ref[...] = jnp.minimum(vmn_ref[...], _fold_rows(lowest, jnp.minimum, SUBLANES))
        cp_ref[...] = cp_ref[...] + _fold_rows(jnp.where(sc > 0.0, 1.0, 0.0), jnp.add, SUBLANES)
        cn_ref[...] = cn_ref[...] + _fold_rows(jnp.where(sc >= 0.0, 1.0, 0.0), jnp.add, SUBLANES)
        return 0

    lax.fori_loop(0, nkb - 1, lambda j, _: score_block(j, False), 0)
    score_block(nkb - 1, True)

    @pl.when(nkb < sc_ref.shape[0])
    def _():
        sc_ref[nkb] = jnp.full((tk, tq), -jnp.inf, F32)

    pairs = _KeyMajor(lambda j: [(sc_ref[2 * j], 2 * j * tk), (sc_ref[2 * j + 1], (2 * j + 1) * tk)],
                      (nkb + 1) // 2, tq)
    nvis = (i * tq + 1 + lax.broadcasted_iota(I32, (1, tq), 1)).astype(F32)
    thr, cidx, need, cut = _topk_select(
        pairs, tq, topk, jnp.min(vmn_ref[...], axis=0, keepdims=True), jnp.max(vmx_ref[...], axis=0, keepdims=True),
        nvis, jnp.sum(cp_ref[...], axis=0, keepdims=True), jnp.sum(cn_ref[...], axis=0, keepdims=True))
    thr_ref[...] = thr
    cidx_ref[...] = cidx

    for half in range(tq // LANES):
        qs = slice(half * LANES, (half + 1) * LANES)

        @pl.when(jnp.max(cut[:, qs]) > 0.5)
        def _(qs=qs):
            def put(j, val):
                sc_ref[j, :, qs] = val
            _drop_excess_ties(lambda j: sc_ref[j, :, qs], put, nkb, tk, LANES, thr[:, qs], need[:, qs], cut[:, qs])

    m_ref[...] = jnp.full(m_ref.shape, NEG, F32)
    acc_ref[...] = jnp.zeros(acc_ref.shape, F32)
    thr = thr_ref[...]
    cidx = cidx_ref[...]

    def attn_block(j, _):
        ks = pl.ds(pl.multiple_of(j * tk, tk), tk)
        bias = jnp.where(_selected(sc_ref[j], key_of(j), thr, cidx), 0.0, NEG).astype(BF16)
        kb = kb_ref[0, ks, :]
        qk = lambda hh: _dot(kb, qt_ref[0, hh])
        ahead = [qk(hh) for hh in range(QK_AHEAD)]
        pending = []
        for hh in range(ATTN_HEADS + PV_DELAY):
            if hh < ATTN_HEADS:
                s = ahead.pop(0).astype(BF16) + bias
                if hh + QK_AHEAD < ATTN_HEADS:
                    ahead.append(qk(hh + QK_AHEAD))
                blk_max = jnp.max(_fold_rows(s, jnp.maximum, 2 * SUBLANES).astype(F32), axis=0, keepdims=True)
                m_old = m_ref[hh]
                m_new = jnp.maximum(m_old, blk_max)
                alpha = jnp.exp2(m_old - m_new)
                p = jnp.exp2(s - m_new.astype(BF16))
                m_ref[hh] = m_new
                pending.append((hh, alpha, p))
            if hh >= PV_DELAY:
                ph, palpha, pp = pending.pop(0)
                acc_ref[ph] = palpha * acc_ref[ph] + _dot(vto_ref[0, ph // KV_GROUP, :, ks], pp)
        return 0

    lax.fori_loop(0, nkb, attn_block, 0)

    outs = []
    for hh in range(ATTN_HEADS):
        acc = acc_ref[hh]
        outs.append((acc[:HEAD_DIM] / acc[HEAD_DIM:]).T)
    o_ref[0] = jnp.concatenate(outs, axis=1).astype(BF16)


def _attn_prompt(qt, qit, wit, kib, kb, vto, topk):
    b, _, _, t = qt.shape
    tq = Q_TILE
    tk = K_BLOCK
    assert t % tq == 0 and t % tk == 0 and tk % tq == 0
    hfeat = lambda n, w: pl.BlockSpec((1, n, w, tq), lambda bi, i: (bi, 0, 0, i))
    seq = lambda w: pl.BlockSpec((1, t, w), lambda bi, i: (bi, 0, 0))
    return pl.pallas_call(
        functools.partial(_attn_prompt_kernel, topk=topk),
        grid=(b, t // tq),
        in_specs=[hfeat(ATTN_HEADS, LANES), hfeat(IDX_HEADS, IDX_DIM),
                  pl.BlockSpec((1, IDX_HEADS, tq), lambda bi, i: (bi, 0, i)), seq(IDX_DIM), seq(LANES),
                  pl.BlockSpec((1, N_KV_HEADS, LANES, t), lambda bi, i: (bi, 0, 0, 0))],
        out_specs=pl.BlockSpec((1, tq, ATTN_WIDTH), lambda bi, i: (bi, i, 0)),
        out_shape=jax.ShapeDtypeStruct((b, t, ATTN_WIDTH), BF16),
        scratch_shapes=[pltpu.VMEM((t // tk + (t // tk) % 2, tk, tq), F32)] + [pltpu.VMEM((SUBLANES, tq), F32)] * 4
                       + [pltpu.VMEM((1, tq), F32), pltpu.VMEM((1, tq), I32),
                          pltpu.VMEM((ATTN_HEADS, 1, tq), F32), pltpu.VMEM((ATTN_HEADS, LANES, tq), F32)],
        compiler_params=pltpu.CompilerParams(dimension_semantics=("arbitrary", "arbitrary"),
                                             vmem_limit_bytes=VMEM_LIMIT),
        name="attn_prompt",
    )(qt, qit, wit, kib, kb, vto)


def _score_sample_kernel(pt_ref, qih_ref, qil_ref, wcol_ref, knh_ref, knl_ref, *rest):
    pages = rest[:PAGES_PER_STEP]
    scp_ref, scn_ref, kh_s, kl_s = rest[PAGES_PER_STEP:]
    c = pl.program_id(1)
    ds = qih_ref.shape[1]
    page = pages[0].shape[1]
    qh = qih_ref[...].reshape(IDX_HEADS * ds, IDX_DIM)
    ql = qil_ref[...].reshape(IDX_HEADS * ds, IDX_DIM)
    wcol = wcol_ref[0]

    def combine(s):
        r = jnp.maximum(s, 0.0) * wcol
        return jnp.sum(r.reshape(IDX_HEADS, ds, r.shape[1]), axis=0)

    for pg in range(PAGES_PER_STEP):
        kh, kl = _split(pages[pg][...])
        kh_s[:, pg * page:(pg + 1) * page] = kh
        kl_s[:, pg * page:(pg + 1) * page] = kl
    kh = kh_s[...]
    scp_ref[0] = combine(_dot(qh, kh) + _dot(ql, kh) + _dot(qh, kl_s[...]))

    @pl.when(c == pl.num_programs(1) - 1)
    def _():
        knh = knh_ref[0]
        sn = combine(_dot_nt(qh, knh) + _dot_nt(ql, knh) + _dot_nt(qh, knl_ref[0]))
        tok = lax.broadcasted_iota(I32, sn.shape, 0)
        col = lax.broadcasted_iota(I32, sn.shape, 1)
        scn_ref[0] = jnp.where(col <= tok, sn, -jnp.inf)


def _select_sample_kernel(scp_ref, scn_ref, thr_ref, cidx_ref, need_ref, cut_ref, *, topk, blk_w):
    rows, past = scp_ref.shape

    def sources(rs):
        return [(lambda j: scp_ref[rs, pl.ds(pl.multiple_of(j * blk_w, blk_w), blk_w)], past // blk_w, blk_w, 0),
                (lambda j: scn_ref[rs, :], 1, scn_ref.shape[1], past)]

    src = sources(slice(None))
    lane_red = lambda fn, op, init, red: red(_row_reduce(src, rows, fn, op, init), axis=1, keepdims=True)
    vmax = lane_red(lambda blk, col: blk, jnp.maximum, -jnp.inf, jnp.max)
    vmin = lane_red(lambda blk, col: jnp.where(blk == -jnp.inf, jnp.inf, blk), jnp.minimum, jnp.inf, jnp.min)
    nvis = _count(src, rows, lambda blk, col: blk > -jnp.inf)
    cpos = _count(src, rows, lambda blk, col: blk > 0.0)
    cnn = _count(src, rows, lambda blk, col: blk >= 0.0)
    thr, cidx, need, cut = _topk_select(src, rows, topk, vmin, vmax, nvis, cpos, cnn)
    thr_ref[...] = thr
    cidx_ref[...] = cidx
    need_ref[...] = need
    cut_ref[...] = cut
    _tie_cutoffs(lambda r0: sources(pl.ds(r0, SUBLANES)), rows, past + scn_ref.shape[1],
                 thr_ref, need_ref, cut_ref, cidx_ref)


def _attn_sample_kernel(pt_ref, qpad_ref, scp_ref, scn_ref, thr_ref, cidx_ref, knew_ref, vnew_ref, *rest,
                        past_len):
    kpages = rest[:PAGES_PER_STEP]
    vpages = rest[PAGES_PER_STEP:2 * PAGES_PER_STEP]
    o_ref, kt_s, vt_s, m_ref, l_ref, acc_ref = rest[2 * PAGES_PER_STEP:]
    c = pl.program_id(1)
    ds = qpad_ref.shape[0]
    page = kpages[0].shape[1]
    thr = thr_ref[...]
    cidx = cidx_ref[...]

    @pl.when(c == 0)
    def _():
        m_ref[...] = jnp.full(m_ref.shape, NEG, F32)
        l_ref[...] = jnp.zeros(l_ref.shape, F32)
        acc_ref[...] = jnp.zeros(acc_ref.shape, F32)

    q_all = jnp.concatenate([qpad_ref[:, hh * LANES:(hh + 1) * LANES] for hh in range(ATTN_HEADS)], axis=0)

    def attend(sc, col, s_of_q, pv):
        bias1 = jnp.where(_selected(sc, col, thr, cidx), 0.0, NEG)
        s = s_of_q + jnp.concatenate([bias1] * ATTN_HEADS, axis=0)
        m_old = m_ref[...]
        m_new = jnp.maximum(m_old, jnp.max(s, axis=1, keepdims=True))
        alpha = jnp.exp2(m_old - m_new)
        p = jnp.exp2(s - m_new)
        l_ref[...] = alpha * l_ref[...] + jnp.sum(p, axis=1, keepdims=True)
        acc_ref[...] = alpha * acc_ref[...] + pv(p.astype(BF16))
        m_ref[...] = m_new

    for pg in range(PAGES_PER_STEP):
        kt_s[:, pg * page:(pg + 1) * page] = kpages[pg][...].astype(BF16)
        vt_s[:, pg * page:(pg + 1) * page] = vpages[pg][...].astype(BF16)
    width = PAGES_PER_STEP * page
    col = c * width + lax.broadcasted_iota(I32, (ds, width), 1)
    attend(scp_ref[0], col, _dot(q_all, kt_s[...]), lambda p: _dot_nt(p, vt_s[...]))

    @pl.when(c == pl.num_programs(1) - 1)
    def _():
        sc = scn_ref[0]
        ncol = past_len + lax.broadcasted_iota(I32, sc.shape, 1)
        attend(sc, ncol, _dot_nt(q_all, knew_ref[0]), lambda p: _dot(p, vnew_ref[0]))
        o = acc_ref[...] / l_ref[...]
        outs = []
        for hh in range(ATTN_HEADS):
            g = hh // KV_GROUP
            outs.append(o[hh * ds:(hh + 1) * ds, g * HEAD_DIM:(g + 1) * HEAD_DIM])
        o_ref[...] = jnp.concatenate(outs, axis=1).astype(BF16)


def _attn_sample(qpad, qih, qil, wi, kih, kil, kb, vb, pool_kt, pool_vt, pool_ikt, page_table, topk, db):
    n = qpad.shape[0]
    ds = n // db
    n_pool, _, page = pool_ikt.shape
    n_pages = page_table.shape[1]
    past_len = n_pages * page
    pps = PAGES_PER_STEP
    n_chunks = n_pages // pps
    assert n_pages % pps == 0 and ds <= LANES

    wcol = wi.reshape(db, ds, IDX_HEADS).transpose(0, 2, 1).reshape(db, IDX_HEADS * ds, 1)
    padk = lambda a: jnp.pad(a.reshape(db, ds, a.shape[1]), ((0, 0), (0, LANES - ds), (0, 0)))
    knh, knl, knew, vnew = padk(kih), padk(kil), padk(kb), padk(vb)

    def page_spec(w, pg):
        return pl.BlockSpec((None, w, page), lambda b, c, pt: (pt[b, c * pps + pg], 0, 0))

    hm = pl.BlockSpec((IDX_HEADS, ds, IDX_DIM), lambda b, c, pt: (0, b, 0))
    seq3 = lambda r, w: pl.BlockSpec((1, r, w), lambda b, c, pt: (b, 0, 0))
    chunk = pl.BlockSpec((1, ds, pps * page), lambda b, c, pt: (b, 0, c))
    scp, scn = pl.pallas_call(
        _score_sample_kernel,
        grid_spec=pltpu.PrefetchScalarGridSpec(
            num_scalar_prefetch=1, grid=(db, n_chunks),
            in_specs=[hm, hm, seq3(IDX_HEADS * ds, 1), seq3(LANES, IDX_DIM), seq3(LANES, IDX_DIM)]
                     + [page_spec(IDX_DIM, pg) for pg in range(pps)],
            out_specs=[chunk, seq3(ds, LANES)],
            scratch_shapes=[pltpu.VMEM((IDX_DIM, pps * page), BF16), pltpu.VMEM((IDX_DIM, pps * page), BF16)]),
        out_shape=[jax.ShapeDtypeStruct((db, ds, past_len), F32), jax.ShapeDtypeStruct((db, ds, LANES), F32)],
        compiler_params=pltpu.CompilerParams(dimension_semantics=("arbitrary", "arbitrary"),
                                             vmem_limit_bytes=VMEM_LIMIT),
        name="score_sample",
    )(page_table, qih, qil, wcol, knh, knl, *([pool_ikt] * pps))

    rows = min(SEL_ROWS, n)
    blk_w = K_BLOCK
    thr, cidx = pl.pallas_call(
        functools.partial(_select_sample_kernel, topk=topk, blk_w=blk_w),
        grid=(n // rows,),
        in_specs=[pl.BlockSpec((rows, past_len), lambda r: (r, 0)), pl.BlockSpec((rows, LANES), lambda r: (r, 0))],
        out_specs=[pl.BlockSpec((rows, 1), lambda r: (r, 0)), pl.BlockSpec((rows, 1), lambda r: (r, 0))],
        out_shape=[jax.ShapeDtypeStruct((n, 1), F32), jax.ShapeDtypeStruct((n, 1), I32)],
        scratch_shapes=[pltpu.VMEM((rows, 1), F32), pltpu.VMEM((rows, 1), F32)],
        compiler_params=pltpu.CompilerParams(dimension_semantics=("arbitrary",), vmem_limit_bytes=VMEM_LIMIT),
        name="select_sample",
    )(scp.reshape(n, past_len), scn.reshape(n, LANES))

    tokrows = lambda w: pl.BlockSpec((ds, w), lambda b, c, pt: (b, 0))
    nrow = ATTN_HEADS * ds
    return pl.pallas_call(
        functools.partial(_attn_sample_kernel, past_len=past_len),
        grid_spec=pltpu.PrefetchScalarGridSpec(
            num_scalar_prefetch=1, grid=(db, n_chunks),
            in_specs=[tokrows(ATTN_HEADS * LANES), chunk, seq3(ds, LANES),
                      tokrows(1), tokrows(1), seq3(LANES, LANES), seq3(LANES, LANES)]
                     + [page_spec(LANES, pg) for pg in range(pps)] * 2,
            out_specs=tokrows(ATTN_WIDTH),
            scratch_shapes=[pltpu.VMEM((LANES, pps * page), BF16), pltpu.VMEM((LANES, pps * page), BF16),
                            pltpu.VMEM((nrow, 1), F32), pltpu.VMEM((nrow, 1), F32),
                            pltpu.VMEM((nrow, LANES), F32)]),
        out_shape=jax.ShapeDtypeStruct((n, ATTN_WIDTH), BF16),
        compiler_params=pltpu.CompilerParams(dimension_semantics=("arbitrary", "arbitrary"),
                                             vmem_limit_bytes=VMEM_LIMIT),
        name="attn_sample",
    )(page_table, qpad, scp, scn, thr, cidx, knew, vnew, *([pool_kt] * pps), *([pool_vt] * pps))


def _mix_mlp_kernel(a_ref, c_ref, x_ref, woa_ref, woc_ref, g1_ref, g2_ref, wup_ref, wdn_ref, g3_ref, y_ref):
    m = _dot(a_ref[...], woa_ref[...]) + _dot(c_ref[...], woc_ref[...])
    x1 = x_ref[...] + _rms(m) * g1_ref[...]
    h = (_rms(x1) * g2_ref[...]).astype(BF16)
    dff = wup_ref.shape[1]
    f = jnp.zeros(x1.shape, F32)
    for cc in range(dff // FF_CHUNK):
        cs = slice(cc * FF_CHUNK, (cc + 1) * FF_CHUNK)
        up = jnp.maximum(_dot(h, wup_ref[:, cs]), 0.0)
        f = f + _dot((up * up).astype(BF16), wdn_ref[cs, :])
    y_ref[...] = x1 + _rms(f) * g3_ref[...]


def _mix_mlp(a, c, x, woa, woc, g1, g2, wup, wdn, g3):
    n, d = x.shape
    tm = min(MLP_TILE, n)
    full = lambda w: pl.BlockSpec(w.shape, lambda i: (0,) * w.ndim)
    tok = lambda w: pl.BlockSpec((tm, w), lambda i: (i, 0))
    return pl.pallas_call(
        _mix_mlp_kernel,
        grid=(n // tm,),
        in_specs=[tok(a.shape[1]), tok(c.shape[1]), tok(d), full(woa), full(woc), full(g1), full(g2),
                  full(wup), full(wdn), full(g3)],
        out_specs=tok(d),
        out_shape=jax.ShapeDtypeStruct((n, d), F32),
        compiler_params=pltpu.CompilerParams(dimension_semantics=("arbitrary",), vmem_limit_bytes=VMEM_LIMIT),
        name="mix_mlp",
    )(a, c, x, woa, woc, g1, g2, wup, wdn, g3)


def _rope_tables(pos):
    rot = HEAD_DIM // 4
    half = rot // 2
    inv_freq = jnp.power(ROPE_THETA, -jnp.arange(half, dtype=F32) * (2.0 / rot))
    dim = np.arange(LANES) % HEAD_DIM
    ang = pos.astype(F32)[:, None] * inv_freq[dim % half][None, :]
    cos, sin = jnp.cos(ang), jnp.sin(ang)
    rc = jnp.where((dim < rot)[None, :], cos, 1.0)
    ra = jnp.where(((dim >= half) & (dim < rot))[None, :], sin, 0.0)
    rb = jnp.where((dim < half)[None, :], -sin, 0.0)
    return rc, ra, rb


def _prep_w_in(w):
    d = w.shape[0]
    sizes = (ATTN_WIDTH, N_KV_HEADS * HEAD_DIM, N_KV_HEADS * HEAD_DIM, IDX_HEADS * IDX_DIM, IDX_DIM, IDX_HEADS)
    offs = np.cumsum((0,) + sizes)
    wq, wk, wv, wqi, wki, wwi = [w[:, offs[i]:offs[i + 1]] for i in range(6)]
    wu = w[:, offs[6]:]
    zpad = jnp.zeros((d, HEAD_DIM), w.dtype)
    qcols = []
    for hh in range(ATTN_HEADS):
        piece = wq[:, hh * HEAD_DIM:(hh + 1) * HEAD_DIM]
        qcols += [piece, zpad] if hh // KV_GROUP == 0 else [zpad, piece]
    wm = jnp.concatenate(qcols + [wk, wv, wu], axis=1).astype(BF16)
    widx = jnp.concatenate([wqi, wki, wwi, jnp.zeros((d, LANES - IDX_DIM - IDX_HEADS), w.dtype)], axis=1)
    wih, wil = _split(widx)
    return wm, wih, wil


def kernel(x_prompt, x_sample, cache_k, cache_v, cache_idx_k, state_conv, page_table, norm_mix_pre, w_in,
           conv_dw_w, conv_dw_b, conv_ln_g, conv_ln_b, w_out, norm_mix_post, norm_mlp_pre, w_up, w_down,
           norm_mlp_post):
    depth = w_in.shape[0]
    b, t, d = x_prompt.shape
    db, ds, _ = x_sample.shape
    n_pool, page = cache_k.shape[1], cache_k.shape[2]
    past_len = page_table.shape[1] * page
    topk_p = min(INDEX_TOPK, t // 4)
    topk_s = min(INDEX_TOPK, (past_len + ds) // 4)
    hist = CONV_WIDTH - 1

    rcp, rap, rbp = _rope_tables(jnp.arange(t, dtype=I32))
    rcs, ras, rbs = [jnp.tile(a, (db, 1)) for a in _rope_tables(past_len + jnp.arange(ds, dtype=I32))]

    xp = x_prompt
    xs = x_sample.reshape(db * ds, d)
    outs = [[] for _ in range(8)]
    row = lambda a: a.reshape(1, -1)
    for l in range(depth):
        wm, wih, wil = _prep_w_in(w_in[l])
        conv_args = (conv_dw_w[l], row(conv_dw_b[l]), row(conv_ln_g[l]), row(conv_ln_b[l]))
        woa = w_out[l, :ATTN_WIDTH].astype(BF16)
        woc = w_out[l, ATTN_WIDTH:].astype(BF16)
        mlp_args = (woa, woc, row(norm_mix_post[l]), row(norm_mlp_pre[l]), w_up[l].astype(BF16),
                    w_down[l].astype(BF16), row(norm_mlp_post[l]))

        (qt, kt32, vt32, kb, vto, kit32, kib, qit, wit, co, ctail) = _inproj_prompt(
            xp, row(norm_mix_pre[l]), wm, wih, wil, rcp, rap, rbp, *conv_args)
        ao = _attn_prompt(qt, qit, wit, kib, kb, vto, topk_p)
        xp = _mix_mlp(ao.reshape(b * t, -1), co.reshape(b * t, -1), xp.reshape(b * t, d), *mlp_args).reshape(b, t, d)
        unt = lambda a: a.reshape(b, N_KV_HEADS, HEAD_DIM, t).transpose(0, 3, 1, 2)
        outs[0].append(unt(kt32))
        outs[1].append(unt(vt32))
        outs[2].append(kit32.transpose(0, 2, 1))
        outs[3].append(ctail[:, HALO - hist:])

        pool_kt = cache_k[l].transpose(0, 2, 3, 1).reshape(n_pool, N_KV_HEADS * HEAD_DIM, page)
        pool_vt = cache_v[l].transpose(0, 2, 3, 1).reshape(n_pool, N_KV_HEADS * HEAD_DIM, page)
        pool_ikt = cache_idx_k[l].transpose(0, 2, 1)
        (qpad, k32, v32, kb, vb, ki32, kih, kil, qih, qil, wi, co, cnew) = _inproj_sample(
            xs, row(norm_mix_pre[l]), wm, wih, wil, rcs, ras, rbs, *conv_args, state_conv[l].transpose(1, 0, 2))
        ao = _attn_sample(qpad, qih, qil, wi, kih, kil, kb, vb, pool_kt, pool_vt, pool_ikt,
                          page_table, topk_s, db)
        xs = _mix_mlp(ao, co, xs, *mlp_args)
        outs[4].append(k32.reshape(db, ds, N_KV_HEADS, HEAD_DIM))
        outs[5].append(v32.reshape(db, ds, N_KV_HEADS, HEAD_DIM))
        outs[6].append(ki32.reshape(db, ds, IDX_DIM))
        outs[7].append(cnew.transpose(1, 0, 2))

    st = [jnp.stack(o, 0) for o in outs]
    return (xp, xs.reshape(db, ds, d), st[0], st[1], st[2], st[3], st[4], st[5], st[6], st[7])
```

```python
import functools
from typing import Any, Callable, NamedTuple

import numpy as np
import jax
import jax.numpy as jnp
from jax import lax
from jax.experimental import pallas as pl
from jax.experimental.pallas import tpu as pltpu

F32 = jnp.float32
BF16 = jnp.bfloat16
I32 = jnp.int32

ATTN_HEADS = 8
HEAD_DIM = 64
ATTN_WIDTH = ATTN_HEADS * HEAD_DIM
N_KV_HEADS = 2
KV_GROUP = ATTN_HEADS // N_KV_HEADS
IDX_HEADS = 8
IDX_DIM = 64
INDEX_TOPK = 256
ROPE_THETA = 500000.0
ROT_HALF = HEAD_DIM // 4 // 2
CONV_WIDTH = 31
RMS_EPS = 1e-6
LN_EPS = 1e-5
SCALE = 0.125
assert HEAD_DIM ** -0.5 == SCALE and IDX_DIM ** -0.5 == SCALE
assert N_KV_HEADS * HEAD_DIM == 128

LANES = 128
SUBLANES = 8
LOG2E = 1.4426950408889634
NEG = -(2.0 ** 100)
HALO = 32
TOK_TILE = 512
Q_TILE = 256
K_BLOCK = 512
SEL_ROWS = 128
PV_DELAY = 2
QK_AHEAD = 3
FOLD_CHAINS = 4
SCAN_CHAINS = 2
BISECT_STEPS = 28
MLP_TILE = 1024
FF_CHUNK = 1024
PAGES_PER_STEP = 64
VMEM_LIMIT = 56 * 1024 * 1024

KEY_NEG_INF = -2139095041
KEY_NAN_HI = 2139095041
INT_MAX = 2147483647


def _dot(a, b):
    return jnp.dot(a, b, preferred_element_type=F32)


def _dot_nt(a, b):
    return lax.dot_general(a, b, (((1,), (1,)), ((), ())), preferred_element_type=F32)


def _split(x):
    hi = x.astype(BF16)
    lo = (x - hi.astype(F32)).astype(BF16)
    return hi, lo


def _rms(x):
    return x * lax.rsqrt(jnp.mean(x * x, axis=-1, keepdims=True) + RMS_EPS)


def _sigmoid(x):
    return 1.0 / (1.0 + jnp.exp(-x))


def _inproj(x, g_pre, wm_ref, wih_ref, wil_ref, rc, ra, rb, on_glu):
    h = _rms(x) * g_pre
    h_hi, h_lo = _split(h)
    nq = ATTN_HEADS * LANES
    nqkv = nq + 2 * LANES
    cch = wm_ref.shape[1] - nqkv
    zu = _dot(h_hi, wm_ref[:, nqkv:])
    u = zu[:, :cch // 2] * _sigmoid(zu[:, cch // 2:])
    on_glu(u)
    zm = _dot(h_hi, wm_ref[:, :nqkv])
    wih = wih_ref[...]
    zi = _dot(h_hi, wih) + _dot(h_lo, wih) + _dot(h_hi, wil_ref[...])

    def rope(c):
        return c * rc + pltpu.roll(c, ROT_HALF, 1) * ra + pltpu.roll(c, LANES - ROT_HALF, 1) * rb

    q = [rope(zm[:, c * LANES:(c + 1) * LANES]) * (SCALE * LOG2E) for c in range(ATTN_HEADS)]
    k = rope(zm[:, nq:nq + LANES])
    v = zm[:, nq + LANES:nq + 2 * LANES]
    qic = [rope(zi[:, c * LANES:(c + 1) * LANES]) * SCALE for c in range(IDX_HEADS * IDX_DIM // LANES)]
    kc = zi[:, IDX_HEADS * IDX_DIM:IDX_HEADS * IDX_DIM + LANES]
    kiw = jnp.where(lax.broadcasted_iota(I32, kc.shape, 1) < IDX_DIM, rope(kc), kc)
    return dict(q=q, k=k, v=v, u=u, qic=qic, kiw=kiw)


def _conv_post(acc, lng, lnb):
    mu = jnp.mean(acc, axis=-1, keepdims=True)
    d = acc - mu
    var = jnp.mean(d * d, axis=-1, keepdims=True)
    y = d * lax.rsqrt(var + LN_EPS) * lng + lnb
    return y * _sigmoid(y)


def _inproj_prompt_kernel(x_ref, g_ref, wm_ref, wih_ref, wil_ref, rc_ref, ra_ref, rb_ref,
                          cw_ref, cb_ref, lng_ref, lnb_ref,
                          qt_ref, kt32_ref, vt32_ref, kb_ref, vto_ref, kit32_ref, kib_ref,
                          qit_ref, wit_ref, co_ref, ctail_ref, ubuf, ush):
    i = pl.program_id(1)
    tt = x_ref.shape[1]

    @pl.when(i == 0)
    def _():
        ubuf[0:HALO, :] = jnp.zeros((HALO, ubuf.shape[1]), F32)

    @pl.when(i > 0)
    def _():
        ubuf[0:HALO, :] = ubuf[tt:tt + HALO, :]

    def conv_branch(u):
        ubuf[HALO:HALO + tt, :] = u
        ctail_ref[0] = ubuf[tt:tt + HALO, :]

        span = ush.shape[1]
        for s in range(1, SUBLANES):
            ush[s - 1] = ubuf[pl.ds(s, span), :]

        rc_rows = 64
        cb = cb_ref[...]
        lng = lng_ref[...]
        lnb = lnb_ref[...]
        off = HALO - (CONV_WIDTH - 1)
        for r in range(tt // rc_rows):
            acc = jnp.broadcast_to(cb, (rc_rows, cb.shape[1]))
            for j in range(CONV_WIDTH):
                a, s = divmod(off + j, SUBLANES)
                rows = pl.ds(SUBLANES * a + r * rc_rows, rc_rows)
                win = ubuf[rows, :] if s == 0 else ush[s - 1, rows, :]
                acc = acc + win * cw_ref[j:j + 1, :]
            co_ref[0, r * rc_rows:(r + 1) * rc_rows, :] = _conv_post(acc, lng, lnb).astype(BF16)

    p = _inproj(x_ref[0], g_ref[...], wm_ref, wih_ref, wil_ref, rc_ref[...], ra_ref[...], rb_ref[...], conv_branch)
    for hh in range(ATTN_HEADS):
        qt_ref[0, hh] = p["q"][hh].T.astype(BF16)
    kt32_ref[0] = p["k"].T
    kb_ref[0] = p["k"].astype(BF16)
    vt = p["v"].T
    vt32_ref[0] = vt
    ones = jnp.ones((HEAD_DIM, tt), F32)
    for g in range(N_KV_HEADS):
        vto_ref[0, g] = jnp.concatenate([vt[g * HEAD_DIM:(g + 1) * HEAD_DIM], ones], axis=0).astype(BF16)
    kiw = p["kiw"]
    kib_ref[0] = kiw[:, :IDX_DIM].astype(BF16)
    kiwt = kiw.T
    kit32_ref[0] = kiwt[:IDX_DIM]
    wit_ref[0] = kiwt[IDX_DIM:IDX_DIM + IDX_HEADS]
    for c, chunk in enumerate(p["qic"]):
        ct = chunk.T.astype(BF16)
        for half in range(LANES // IDX_DIM):
            qit_ref[0, c * (LANES // IDX_DIM) + half] = ct[half * IDX_DIM:(half + 1) * IDX_DIM]


def _inproj_sample_kernel(x_ref, g_ref, wm_ref, wih_ref, wil_ref, rc_ref, ra_ref, rb_ref,
                          cw_ref, cb_ref, lng_ref, lnb_ref, st_ref,
                          qpad_ref, k32_ref, v32_ref, kb_ref, vb_ref, ki32_ref, kih_ref, kil_ref,
                          qih_ref, qil_ref, wi_ref, co_ref, cnew_ref, fbuf, uscr):
    hist, db, cch = st_ref.shape
    ds = x_ref.shape[0] // db

    def conv_branch(u):
        nch = cch // LANES
        for cc in range(nch):
            uscr[cc] = u[:, cc * LANES:(cc + 1) * LANES]
        fbuf[0:hist] = st_ref[...]
        for t in range(ds):
            for cc in range(nch):
                fbuf[hist + t, :, cc * LANES:(cc + 1) * LANES] = uscr[cc, pl.ds(t, db, stride=ds), :]
        acc = jnp.broadcast_to(cb_ref[...].reshape(1, 1, cch), (ds, db, cch))
        for j in range(CONV_WIDTH):
            acc = acc + fbuf[j:j + ds] * cw_ref[j:j + 1, :].reshape(1, 1, cch)
        y = _conv_post(acc.reshape(ds * db, cch), lng_ref[...], lnb_ref[...])
        for t in range(ds):
            for cc in range(nch):
                uscr[cc, pl.ds(t, db, stride=ds), :] = y[t * db:(t + 1) * db, cc * LANES:(cc + 1) * LANES]
        for cc in range(nch):
            co_ref[:, cc * LANES:(cc + 1) * LANES] = uscr[cc].astype(BF16)
        cnew_ref[...] = fbuf[ds:ds + hist]

    p = _inproj(x_ref[...], g_ref[...], wm_ref, wih_ref, wil_ref, rc_ref[...], ra_ref[...], rb_ref[...], conv_branch)
    for hh in range(ATTN_HEADS):
        qpad_ref[:, hh * LANES:(hh + 1) * LANES] = p["q"][hh].astype(BF16)
    k32_ref[...] = p["k"]
    v32_ref[...] = p["v"]
    kb_ref[...] = p["k"].astype(BF16)
    vb_ref[...] = p["v"].astype(BF16)
    ki = p["kiw"][:, :IDX_DIM]
    ki32_ref[...] = ki
    kih, kil = _split(ki)
    kih_ref[...] = kih
    kil_ref[...] = kil
    wi_ref[...] = p["kiw"][:, IDX_DIM:IDX_DIM + IDX_HEADS]
    per_chunk = LANES // IDX_DIM
    for hh in range(IDX_HEADS):
        half = hh % per_chunk
        hi, lo = _split(p["qic"][hh // per_chunk][:, half * IDX_DIM:(half + 1) * IDX_DIM])
        qih_ref[hh] = hi
        qil_ref[hh] = lo


def _inproj_prompt(x, g_pre, wm, wih, wil, rc, ra, rb, cw, cb, lng, lnb):
    b, t, d = x.shape
    tt = TOK_TILE
    cch = cw.shape[1]
    full = lambda a: pl.BlockSpec(a.shape, lambda bi, i: (0,) * a.ndim)
    tok = lambda w: pl.BlockSpec((1, tt, w), lambda bi, i: (bi, i, 0))
    feat = lambda w: pl.BlockSpec((1, w, tt), lambda bi, i: (bi, 0, i))
    tab = pl.BlockSpec((tt, LANES), lambda bi, i: (i, 0))
    hfeat = lambda n, w: pl.BlockSpec((1, n, w, tt), lambda bi, i: (bi, 0, 0, i))
    sds = jax.ShapeDtypeStruct
    out_shape = [sds((b, ATTN_HEADS, LANES, t), BF16), sds((b, LANES, t), F32), sds((b, LANES, t), F32),
                 sds((b, t, LANES), BF16), sds((b, N_KV_HEADS, LANES, t), BF16),
                 sds((b, IDX_DIM, t), F32), sds((b, t, IDX_DIM), BF16),
                 sds((b, IDX_HEADS, IDX_DIM, t), BF16),
                 sds((b, IDX_HEADS, t), F32), sds((b, t, cch), BF16), sds((b, HALO, cch), F32)]
    out_specs = [hfeat(ATTN_HEADS, LANES), feat(LANES), feat(LANES), tok(LANES), hfeat(N_KV_HEADS, LANES),
                 feat(IDX_DIM), tok(IDX_DIM), hfeat(IDX_HEADS, IDX_DIM), feat(IDX_HEADS), tok(cch),
                 pl.BlockSpec((1, HALO, cch), lambda bi, i: (bi, 0, 0))]
    return pl.pallas_call(
        _inproj_prompt_kernel,
        grid=(b, t // tt),
        in_specs=[tok(d), full(g_pre), full(wm), full(wih), full(wil), tab, tab, tab,
                  full(cw), full(cb), full(lng), full(lnb)],
        out_specs=out_specs,
        out_shape=out_shape,
        scratch_shapes=[pltpu.VMEM((tt + HALO, cch), F32), pltpu.VMEM((SUBLANES - 1, tt + HALO - SUBLANES, cch), F32)],
        compiler_params=pltpu.CompilerParams(dimension_semantics=("arbitrary", "arbitrary"),
                                             vmem_limit_bytes=VMEM_LIMIT),
        name="inproj_prompt",
    )(x, g_pre, wm, wih, wil, rc, ra, rb, cw, cb, lng, lnb)


def _inproj_sample(x, g_pre, wm, wih, wil, rc, ra, rb, cw, cb, lng, lnb, state):
    n, d = x.shape
    hist, db, cch = state.shape
    ds = n // db
    sds = jax.ShapeDtypeStruct
    out_shape = [sds((n, ATTN_HEADS * LANES), BF16), sds((n, LANES), F32), sds((n, LANES), F32),
                 sds((n, LANES), BF16), sds((n, LANES), BF16),
                 sds((n, IDX_DIM), F32), sds((n, IDX_DIM), BF16), sds((n, IDX_DIM), BF16),
                 sds((IDX_HEADS, n, IDX_DIM), BF16), sds((IDX_HEADS, n, IDX_DIM), BF16),
                 sds((n, IDX_HEADS), F32), sds((n, cch), BF16), sds((hist, db, cch), F32)]
    return pl.pallas_call(
        _inproj_sample_kernel,
        out_shape=out_shape,
        scratch_shapes=[pltpu.VMEM((hist + ds, db, cch), F32), pltpu.VMEM((cch // LANES, n, LANES), F32)],
        compiler_params=pltpu.CompilerParams(vmem_limit_bytes=VMEM_LIMIT),
        name="inproj_sample",
    )(x, g_pre, wm, wih, wil, rc, ra, rb, cw, cb, lng, lnb, state)


def _key_to_f32(key):
    bits = jnp.where(key < 0, key ^ INT_MAX, key)
    return lax.bitcast_convert_type(bits, F32)


def _row_reduce(sources, rows, fn, op, init):
    c = jnp.full((rows, LANES), init, F32)
    for get, nblk, width, col0 in sources:
        def body(j, c, get=get, width=width, col0=col0):
            blk = get(j)
            col = col0 + j * width + lax.broadcasted_iota(I32, blk.shape, 1)
            val = fn(blk, col)
            for cc in range(width // LANES):
                c = op(c, val[:, cc * LANES:(cc + 1) * LANES])
            return c
        c = lax.fori_loop(0, nblk, body, c)
    return c


def _fold_rows(x, op, part, chains=FOLD_CHAINS):
    groups = [x[r:r + part] for r in range(0, x.shape[0], part)]
    parts = groups[:chains]
    for k, g in enumerate(groups[chains:]):
        parts[k % chains] = op(parts[k % chains], g)
    while len(parts) > 1:
        parts = [op(parts[k], parts[k + 1]) if k + 1 < len(parts) else parts[k] for k in range(0, len(parts), 2)]
    return parts[0]


class _KeyMajor(NamedTuple):
    get: Callable
    nblk: Any
    q: int


def _vshape(sources, rows):
    return (1, sources.q) if isinstance(sources, _KeyMajor) else (rows, 1)


def _scan(sources, rows, fn, op, red, init):
    if not isinstance(sources, _KeyMajor):
        return red(_row_reduce(sources, rows, fn, op, init), axis=1, keepdims=True)
    def body(j, c):
        for blk, key0 in sources.get(j):
            key = key0 + lax.broadcasted_iota(I32, blk.shape, 0)
            c = op(c, _fold_rows(fn(blk, key), op, SUBLANES, SCAN_CHAINS))
        return c

    c = lax.fori_loop(0, sources.nblk, body, jnp.full((SUBLANES, sources.q), init, F32))
    return red(c, axis=0, keepdims=True)


def _count(sources, rows, pred):
    return _scan(sources, rows, lambda blk, col: jnp.where(pred(blk, col), 1.0, 0.0), jnp.add, jnp.sum, 0.0)


def _topk_select(sources, rows, topk, vmin, vmax, nvis, cpos, cnn):
    kf = float(topk)
    few = nvis <= kf
    zero_top = (cpos <= kf) & (cnn >= kf) & jnp.logical_not(few)
    up = cpos > kf

    def unsettled(carry):
        return (carry[0] < BISECT_STEPS) & (jnp.min(carry[6]) < 0.5)

    logit = lambda c: jnp.log((c + 0.5) / (nvis - c + 0.5))
    z_k = logit(kf)

    def step(carry):
        it, lo, hi, clo, chi, thr, done = carry
        z_lo, z_hi = logit(clo), logit(chi)
        den = z_lo - z_hi
        t = jnp.clip(jnp.where(den > 0.0, (z_lo - z_k) / den, 0.5), 1.0 / 32.0, 31.0 / 32.0)
        t = jnp.where((it & 3) == 3, 0.5, t)
        mid = lo + (hi - lo) * t
        c = _count(sources, rows, lambda blk, col: blk > mid)
        hit = (c == kf) & (done < 0.5)
        above, below = c > kf, c < kf
        return (it + 1, jnp.where(above, mid, lo), jnp.where(below, mid, hi),
                jnp.where(above, c, clo), jnp.where(below, c, chi),
                jnp.where(hit, mid, thr), jnp.where(hit, 1.0, done))

    carry0 = (jnp.int32(0), jnp.where(up, 0.0, vmin), jnp.where(up, vmax, 0.0),
              jnp.where(up, cpos, nvis), jnp.where(up, 0.0, cpos),
              jnp.where(few, -jnp.inf, 0.0), jnp.where(few | zero_top, 1.0, 0.0))
    thr, done = lax.while_loop(unsettled, step, carry0)[5:]
    settled = done > 0.5
    zero_need = kf - cpos
    cidx = jnp.where(zero_top & (zero_need > 0.0), INT_MAX, -1)
    need = jnp.where(zero_top, zero_need, 0.0)
    cut = jnp.where(zero_top & (cnn > kf) & (zero_need > 0.0), 1.0, 0.0)

    def exact(_):
        thr_e, cnt_e = _kth_largest_exact(sources, rows, topk)
        need_e = kf - _count(sources, rows, lambda blk, col: blk > thr_e)
        fin = thr_e > -jnp.inf
        return (jnp.where(settled, thr, thr_e), jnp.where(settled, cidx, jnp.where(fin, INT_MAX, -1)),
                jnp.where(settled, need, need_e), jnp.where(settled, cut, jnp.where(fin & (cnt_e > kf), 1.0, 0.0)))

    return lax.cond(jnp.min(done) < 0.5, exact, lambda _: (thr, cidx, need, cut), 0)


def _tie_cutoffs(slab_sources, nrows, ncols_max, thr_ref, need_ref, cut_ref, cidx_ref):
    n_it = int(np.ceil(np.log2(ncols_max + 1))) + 1
    sr = SUBLANES

    def slab(sl, _):
        r0 = pl.multiple_of(sl * sr, sr)
        rs = pl.ds(r0, sr)
        cut = cut_ref[rs, :] > 0.5

        @pl.when(jnp.max(cut_ref[rs, :]) > 0.5)
        def _():
            thr = thr_ref[rs, :]
            need = need_ref[rs, :]
            src = slab_sources(r0)

            def bis_c(_, carry):
                lo_c, hi_c = carry
                mid = (lo_c + hi_c) >> 1
                ge = _count(src, sr, lambda blk, col: (blk == thr) & (col <= mid)) >= need
                return jnp.where(ge, lo_c, mid), jnp.where(ge, mid, hi_c)

            _, hi_c = lax.fori_loop(0, n_it, bis_c, (jnp.full((sr, 1), -1, I32),
                                                     jnp.full((sr, 1), ncols_max - 1, I32)))
            cidx_ref[rs, :] = jnp.where(cut, hi_c, cidx_ref[rs, :])
        return 0

    lax.fori_loop(0, nrows // sr, slab, 0)


def _drop_excess_ties(get_blk, put_blk, nblk, width, q, thr, need, cut):
    tri = jnp.where(lax.broadcasted_iota(I32, (width, width), 1) <= lax.broadcasted_iota(I32, (width, width), 0),
                    1.0, 0.0).astype(BF16)
    marked = cut > 0.5

    def body(j, before):
        blk = get_blk(j)
        tied = (blk == thr) & marked
        rank = _dot(tri, jnp.where(tied, 1.0, 0.0).astype(BF16))
        put_blk(j, jnp.where(tied & (before + rank > need), -jnp.inf, blk))
        return before + jnp.max(rank, axis=0, keepdims=True)

    lax.fori_loop(0, nblk, body, jnp.zeros((1, q), F32))


def _kth_largest_exact(sources, rows, topk):
    kf = float(topk)

    def bis(_, carry):
        lo, hi = carry
        mid = (lo >> 1) + (hi >> 1) + (lo & hi & 1)
        thr = _key_to_f32(mid)
        ge = _count(sources, rows, lambda blk, col: blk >= thr) >= kf
        return jnp.where(ge, mid, lo), jnp.where(ge, hi, mid)

    lo0 = jnp.full(_vshape(sources, rows), KEY_NEG_INF, I32)
    hi0 = jnp.full(_vshape(sources, rows), KEY_NAN_HI, I32)
    _, hi = lax.fori_loop(0, 32, bis, (lo0, hi0))

    def below(upper):
        m = _scan(sources, rows, lambda blk, col: jnp.where(blk < upper, blk, -jnp.inf),
                  jnp.maximum, jnp.max, -jnp.inf)
        return m, _count(sources, rows, lambda blk, col: blk >= m)

    def short(carry):
        return jnp.max(jnp.where(carry[2] < kf, 1.0, 0.0)) > 0.0

    def lower(carry):
        upper, m, c = carry
        upper = jnp.where(c < kf, m, upper)
        m, c = below(upper)
        return upper, m, c

    upper0 = _key_to_f32(hi)
    m0, c0 = below(upper0)
    _, thr, cnt = lax.while_loop(short, lower, (upper0, m0, c0))
    return thr, cnt


def _selected(blk, col, thr, cidx):
    return (blk > thr) | ((blk == thr) & (col <= cidx))


def _attn_prompt_kernel(qt_ref, qit_ref, wit_ref, kib_ref, kb_ref, vto_ref, o_ref,
                        sc_ref, vmx_ref, vmn_ref, cp_ref, cn_ref, thr_ref, cidx_ref, m_ref, acc_ref, *, topk):
    i = pl.program_id(1)
    tq = qt_ref.shape[3]
    tk = sc_ref.shape[1]
    nkb = (i * tq + tq + tk - 1) // tk
    q_pos = i * tq + lax.broadcasted_iota(I32, (tk, tq), 1)
    key_of = lambda j: j * tk + lax.broadcasted_iota(I32, (tk, tq), 0)
    wit = wit_ref[0]

    vmx_ref[...] = jnp.full(vmx_ref.shape, -jnp.inf, F32)
    vmn_ref[...] = jnp.full(vmn_ref.shape, jnp.inf, F32)
    cp_ref[...] = jnp.zeros(cp_ref.shape, F32)
    cn_ref[...] = jnp.zeros(cn_ref.shape, F32)

    def score_block(j, causal):
        kib = kib_ref[0, pl.ds(pl.multiple_of(j * tk, tk), tk), :]
        acc = jnp.zeros((tk, tq), F32)
        for hh in range(IDX_HEADS):
            acc = acc + jnp.maximum(_dot(kib, qit_ref[0, hh]), 0.0) * wit[hh:hh + 1, :]
        if causal:
            vis = key_of(j) <= q_pos
            sc = jnp.where(vis, acc, -jnp.inf)
            lowest = jnp.where(vis, acc, jnp.inf)
        else:
            sc = lowest = acc
        sc_ref[j] = sc
        vmx_ref[...] = jnp.maximum(vmx_ref[...], _fold_rows(sc, jnp.maximum, SUBLANES))
        vmn_ref[...] = jnp.minimum(vmn_ref[...], _fold_rows(lowest, jnp.minimum, SUBLANES))
        cp_ref[...] = cp_ref[...] + _fold_rows(jnp.where(sc > 0.0, 1.0, 0.0), jnp.add, SUBLANES)
        cn_ref[...] = cn_ref[...] + _fold_rows(jnp.where(sc >= 0.0, 1.0, 0.0), jnp.add, SUBLANES)
        return 0

    lax.fori_loop(0, nkb - 1, lambda j, _: score_block(j, False), 0)
    score_block(nkb - 1, True)

    @pl.when(nkb < sc_ref.shape[0])
    def _():
        sc_ref[nkb] = jnp.full((tk, tq), -jnp.inf, F32)

    pairs = _KeyMajor(lambda j: [(sc_ref[2 * j], 2 * j * tk), (sc_ref[2 * j + 1], (2 * j + 1) * tk)],
                      (nkb + 1) // 2, tq)
    nvis = (i * tq + 1 + lax.broadcasted_iota(I32, (1, tq), 1)).astype(F32)
    thr, cidx, need, cut = _topk_select(
        pairs, tq, topk, jnp.min(vmn_ref[...], axis=0, keepdims=True), jnp.max(vmx_ref[...], axis=0, keepdims=True),
        nvis, jnp.sum(cp_ref[...], axis=0, keepdims=True), jnp.sum(cn_ref[...], axis=0, keepdims=True))
    thr_ref[...] = thr
    cidx_ref[...] = cidx

    for half in range(tq // LANES):
        qs = slice(half * LANES, (half + 1) * LANES)

        @pl.when(jnp.max(cut[:, qs]) > 0.5)
        def _(qs=qs):
            def put(j, val):
                sc_ref[j, :, qs] = val
            _drop_excess_ties(lambda j: sc_ref[j, :, qs], put, nkb, tk, LANES, thr[:, qs], need[:, qs], cut[:, qs])

    m_ref[...] = jnp.full(m_ref.shape, NEG, F32)
    acc_ref[...] = jnp.zeros(acc_ref.shape, F32)
    thr = thr_ref[...]
    cidx = cidx_ref[...]

    def attn_block(j, _):
        ks = pl.ds(pl.multiple_of(j * tk, tk), tk)
        bias = jnp.where(_selected(sc_ref[j], key_of(j), thr, cidx), 0.0, NEG).astype(BF16)
        kb = kb_ref[0, ks, :]
        qk = lambda hh: _dot(kb, qt_ref[0, hh])
        ahead = [qk(hh) for hh in range(QK_AHEAD)]
        pending = []
        for hh in range(ATTN_HEADS + PV_DELAY):
            if hh < ATTN_HEADS:
                s = ahead.pop(0).astype(BF16) + bias
                if hh + QK_AHEAD < ATTN_HEADS:
                    ahead.append(qk(hh + QK_AHEAD))
                blk_max = jnp.max(_fold_rows(s, jnp.maximum, 2 * SUBLANES).astype(F32), axis=0, keepdims=True)
                m_old = m_ref[hh]
                m_new = jnp.maximum(m_old, blk_max)
                alpha = jnp.exp2(m_old - m_new)
                p = jnp.exp2(s - m_new.astype(BF16))
                m_ref[hh] = m_new
                pending.append((hh, alpha, p))
            if hh >= PV_DELAY:
                ph, palpha, pp = pending.pop(0)
                acc_ref[ph] = palpha * acc_ref[ph] + _dot(vto_ref[0, ph // KV_GROUP, :, ks], pp)
        return 0

    lax.fori_loop(0, nkb, attn_block, 0)

    outs = []
    for hh in range(ATTN_HEADS):
        acc = acc_ref[hh]
        outs.append((acc[:HEAD_DIM] / acc[HEAD_DIM:]).T)
    o_ref[0] = jnp.concatenate(outs, axis=1).astype(BF16)


def _attn_prompt(qt, qit, wit, kib, kb, vto, topk):
    b, _, _, t = qt.shape
    tq = Q_TILE
    tk = K_BLOCK
    assert t % tq == 0 and t % tk == 0 and tk % tq == 0
    hfeat = lambda n, w: pl.BlockSpec((1, n, w, tq), lambda bi, i: (bi, 0, 0, i))
    seq = lambda w: pl.BlockSpec((1, t, w), lambda bi, i: (bi, 0, 0))
    return pl.pallas_call(
        functools.partial(_attn_prompt_kernel, topk=topk),
        grid=(b, t // tq),
        in_specs=[hfeat(ATTN_HEADS, LANES), hfeat(IDX_HEADS, IDX_DIM),
                  pl.BlockSpec((1, IDX_HEADS, tq), lambda bi, i: (bi, 0, i)), seq(IDX_DIM), seq(LANES),
                  pl.BlockSpec((1, N_KV_HEADS, LANES, t), lambda bi, i: (bi, 0, 0, 0))],
        out_specs=pl.BlockSpec((1, tq, ATTN_WIDTH), lambda bi, i: (bi, i, 0)),
        out_shape=jax.ShapeDtypeStruct((b, t, ATTN_WIDTH), BF16),
        scratch_shapes=[pltpu.VMEM((t // tk + (t // tk) % 2, tk, tq), F32)] + [pltpu.VMEM((SUBLANES, tq), F32)] * 4
                       + [pltpu.VMEM((1, tq), F32), pltpu.VMEM((1, tq), I32),
                          pltpu.VMEM((ATTN_HEADS, 1, tq), F32), pltpu.VMEM((ATTN_HEADS, LANES, tq), F32)],
        compiler_params=pltpu.CompilerParams(dimension_semantics=("arbitrary", "arbitrary"),
                                             vmem_limit_bytes=VMEM_LIMIT),
        name="attn_prompt",
    )(qt, qit, wit, kib, kb, vto)


def _score_sample_kernel(pt_ref, qih_ref, qil_ref, wcol_ref, knh_ref, knl_ref, *rest):
    pages = rest[:PAGES_PER_STEP]
    scp_ref, scn_ref, kh_s, kl_s = rest[PAGES_PER_STEP:]
    c = pl.program_id(1)
    ds = qih_ref.shape[1]
    page = pages[0].shape[1]
    qh = qih_ref[...].reshape(IDX_HEADS * ds, IDX_DIM)
    ql = qil_ref[...].reshape(IDX_HEADS * ds, IDX_DIM)
    wcol = wcol_ref[0]

    def combine(s):
        r = jnp.maximum(s, 0.0) * wcol
        return jnp.sum(r.reshape(IDX_HEADS, ds, r.shape[1]), axis=0)

    for pg in range(PAGES_PER_STEP):
        kh, kl = _split(pages[pg][...])
        kh_s[:, pg * page:(pg + 1) * page] = kh
        kl_s[:, pg * page:(pg + 1) * page] = kl
    kh = kh_s[...]
    scp_ref[0] = combine(_dot(qh, kh) + _dot(ql, kh) + _dot(qh, kl_s[...]))

    @pl.when(c == pl.num_programs(1) - 1)
    def _():
        knh = knh_ref[0]
        sn = combine(_dot_nt(qh, knh) + _dot_nt(ql, knh) + _dot_nt(qh, knl_ref[0]))
        tok = lax.broadcasted_iota(I32, sn.shape, 0)
        col = lax.broadcasted_iota(I32, sn.shape, 1)
        scn_ref[0] = jnp.where(col <= tok, sn, -jnp.inf)


def _select_sample_kernel(scp_ref, scn_ref, thr_ref, cidx_ref, need_ref, cut_ref, *, topk, blk_w):
    rows, past = scp_ref.shape

    def sources(rs):
        return [(lambda j: scp_ref[rs, pl.ds(pl.multiple_of(j * blk_w, blk_w), blk_w)], past // blk_w, blk_w, 0),
                (lambda j: scn_ref[rs, :], 1, scn_ref.shape[1], past)]

    src = sources(slice(None))
    lane_red = lambda fn, op, init, red: red(_row_reduce(src, rows, fn, op, init), axis=1, keepdims=True)
    vmax = lane_red(lambda blk, col: blk, jnp.maximum, -jnp.inf, jnp.max)
    vmin = lane_red(lambda blk, col: jnp.where(blk == -jnp.inf, jnp.inf, blk), jnp.minimum, jnp.inf, jnp.min)
    nvis = _count(src, rows, lambda blk, col: blk > -jnp.inf)
    cpos = _count(src, rows, lambda blk, col: blk > 0.0)
    cnn = _count(src, rows, lambda blk, col: blk >= 0.0)
    thr, cidx, need, cut = _topk_select(src, rows, topk, vmin, vmax, nvis, cpos, cnn)
    thr_ref[...] = thr
    cidx_ref[...] = cidx
    need_ref[...] = need
    cut_ref[...] = cut
    _tie_cutoffs(lambda r0: sources(pl.ds(r0, SUBLANES)), rows, past + scn_ref.shape[1],
                 thr_ref, need_ref, cut_ref, cidx_ref)


def _attn_sample_kernel(pt_ref, qpad_ref, scp_ref, scn_ref, thr_ref, cidx_ref, knew_ref, vnew_ref, *rest,
                        past_len):
    kpages = rest[:PAGES_PER_STEP]
    vpages = rest[PAGES_PER_STEP:2 * PAGES_PER_STEP]
    o_ref, kt_s, vt_s, m_ref, l_ref, acc_ref = rest[2 * PAGES_PER_STEP:]
    c = pl.program_id(1)
    ds = qpad_ref.shape[0]
    page = kpages[0].shape[1]
    thr = thr_ref[...]
    cidx = cidx_ref[...]

    @pl.when(c == 0)
    def _():
        m_ref[...] = jnp.full(m_ref.shape, NEG, F32)
        l_ref[...] = jnp.zeros(l_ref.shape, F32)
        acc_ref[...] = jnp.zeros(acc_ref.shape, F32)

    q_all = jnp.concatenate([qpad_ref[:, hh * LANES:(hh + 1) * LANES] for hh in range(ATTN_HEADS)], axis=0)

    def attend(sc, col, s_of_q, pv):
        bias1 = jnp.where(_selected(sc, col, thr, cidx), 0.0, NEG)
        s = s_of_q + jnp.concatenate([bias1] * ATTN_HEADS, axis=0)
        m_old = m_ref[...]
        m_new = jnp.maximum(m_old, jnp.max(s, axis=1, keepdims=True))
        alpha = jnp.exp2(m_old - m_new)
        p = jnp.exp2(s - m_new)
        l_ref[...] = alpha * l_ref[...] + jnp.sum(p, axis=1, keepdims=True)
        acc_ref[...] = alpha * acc_ref[...] + pv(p.astype(BF16))
        m_ref[...] = m_new

    for pg in range(PAGES_PER_STEP):
        kt_s[:, pg * page:(pg + 1) * page] = kpages[pg][...].astype(BF16)
        vt_s[:, pg * page:(pg + 1) * page] = vpages[pg][...].astype(BF16)
    width = PAGES_PER_STEP * page
    col = c * width + lax.broadcasted_iota(I32, (ds, width), 1)
    attend(scp_ref[0], col, _dot(q_all, kt_s[...]), lambda p: _dot_nt(p, vt_s[...]))

    @pl.when(c == pl.num_programs(1) - 1)
    def _():
        sc = scn_ref[0]
        ncol = past_len + lax.broadcasted_iota(I32, sc.shape, 1)
        attend(sc, ncol, _dot_nt(q_all, knew_ref[0]), lambda p: _dot(p, vnew_ref[0]))
        o = acc_ref[...] / l_ref[...]
        outs = []
        for hh in range(ATTN_HEADS):
            g = hh // KV_GROUP
            outs.append(o[hh * ds:(hh + 1) * ds, g * HEAD_DIM:(g + 1) * HEAD_DIM])
        o_ref[...] = jnp.concatenate(outs, axis=1).astype(BF16)


def _attn_sample(qpad, qih, qil, wi, kih, kil, kb, vb, pool_kt, pool_vt, pool_ikt, page_table, topk, db):
    n = qpad.shape[0]
    ds = n // db
    n_pool, _, page = pool_ikt.shape
    n_pages = page_table.shape[1]
    past_len = n_pages * page
    pps = PAGES_PER_STEP
    n_chunks = n_pages // pps
    assert n_pages % pps == 0 and ds <= LANES

    wcol = wi.reshape(db, ds, IDX_HEADS).transpose(0, 2, 1).reshape(db, IDX_HEADS * ds, 1)
    padk = lambda a: jnp.pad(a.reshape(db, ds, a.shape[1]), ((0, 0), (0, LANES - ds), (0, 0)))
    knh, knl, knew, vnew = padk(kih), padk(kil), padk(kb), padk(vb)

    def page_spec(w, pg):
        return pl.BlockSpec((None, w, page), lambda b, c, pt: (pt[b, c * pps + pg], 0, 0))

    hm = pl.BlockSpec((IDX_HEADS, ds, IDX_DIM), lambda b, c, pt: (0, b, 0))
    seq3 = lambda r, w: pl.BlockSpec((1, r, w), lambda b, c, pt: (b, 0, 0))
    chunk = pl.BlockSpec((1, ds, pps * page), lambda b, c, pt: (b, 0, c))
    scp, scn = pl.pallas_call(
        _score_sample_kernel,
        grid_spec=pltpu.PrefetchScalarGridSpec(
            num_scalar_prefetch=1, grid=(db, n_chunks),
            in_specs=[hm, hm, seq3(IDX_HEADS * ds, 1), seq3(LANES, IDX_DIM), seq3(LANES, IDX_DIM)]
                     + [page_spec(IDX_DIM, pg) for pg in range(pps)],
            out_specs=[chunk, seq3(ds, LANES)],
            scratch_shapes=[pltpu.VMEM((IDX_DIM, pps * page), BF16), pltpu.VMEM((IDX_DIM, pps * page), BF16)]),
        out_shape=[jax.ShapeDtypeStruct((db, ds, past_len), F32), jax.ShapeDtypeStruct((db, ds, LANES), F32)],
        compiler_params=pltpu.CompilerParams(dimension_semantics=("arbitrary", "arbitrary"),
                                             vmem_limit_bytes=VMEM_LIMIT),
        name="score_sample",
    )(page_table, qih, qil, wcol, knh, knl, *([pool_ikt] * pps))

    rows = min(SEL_ROWS, n)
    blk_w = K_BLOCK
    thr, cidx = pl.pallas_call(
        functools.partial(_select_sample_kernel, topk=topk, blk_w=blk_w),
        grid=(n // rows,),
        in_specs=[pl.BlockSpec((rows, past_len), lambda r: (r, 0)), pl.BlockSpec((rows, LANES), lambda r: (r, 0))],
        out_specs=[pl.BlockSpec((rows, 1), lambda r: (r, 0)), pl.BlockSpec((rows, 1), lambda r: (r, 0))],
        out_shape=[jax.ShapeDtypeStruct((n, 1), F32), jax.ShapeDtypeStruct((n, 1), I32)],
        scratch_shapes=[pltpu.VMEM((rows, 1), F32), pltpu.VMEM((rows, 1), F32)],
        compiler_params=pltpu.CompilerParams(dimension_semantics=("arbitrary",), vmem_limit_bytes=VMEM_LIMIT),
        name="select_sample",
    )(scp.reshape(n, past_len), scn.reshape(n, LANES))

    tokrows = lambda w: pl.BlockSpec((ds, w), lambda b, c, pt: (b, 0))
    nrow = ATTN_HEADS * ds
    return pl.pallas_call(
        functools.partial(_attn_sample_kernel, past_len=past_len),
        grid_spec=pltpu.PrefetchScalarGridSpec(
            num_scalar_prefetch=1, grid=(db, n_chunks),
            in_specs=[tokrows(ATTN_HEADS * LANES), chunk, seq3(ds, LANES),
                      tokrows(1), tokrows(1), seq3(LANES, LANES), seq3(LANES, LANES)]
                     + [page_spec(LANES, pg) for pg in range(pps)] * 2,
            out_specs=tokrows(ATTN_WIDTH),
            scratch_shapes=[pltpu.VMEM((LANES, pps * page), BF16), pltpu.VMEM((LANES, pps * page), BF16),
                            pltpu.VMEM((nrow, 1), F32), pltpu.VMEM((nrow, 1), F32),
                            pltpu.VMEM((nrow, LANES), F32)]),
        out_shape=jax.ShapeDtypeStruct((n, ATTN_WIDTH), BF16),
        compiler_params=pltpu.CompilerParams(dimension_semantics=("arbitrary", "arbitrary"),
                                             vmem_limit_bytes=VMEM_LIMIT),
        name="attn_sample",
    )(page_table, qpad, scp, scn, thr, cidx, knew, vnew, *([pool_kt] * pps), *([pool_vt] * pps))


def _mix_mlp_kernel(a_ref, c_ref, x_ref, woa_ref, woc_ref, g1_ref, g2_ref, wup_ref, wdn_ref, g3_ref, y_ref):
    m = _dot(a_ref[...], woa_ref[...]) + _dot(c_ref[...], woc_ref[...])
    x1 = x_ref[...] + _rms(m) * g1_ref[...]
    h = (_rms(x1) * g2_ref[...]).astype(BF16)
    dff = wup_ref.shape[1]
    f = jnp.zeros(x1.shape, F32)
    for cc in range(dff // FF_CHUNK):
        cs = slice(cc * FF_CHUNK, (cc + 1) * FF_CHUNK)
        up = jnp.maximum(_dot(h, wup_ref[:, cs]), 0.0)
        f = f + _dot((up * up).astype(BF16), wdn_ref[cs, :])
    y_ref[...] = x1 + _rms(f) * g3_ref[...]


def _mix_mlp(a, c, x, woa, woc, g1, g2, wup, wdn, g3):
    n, d = x.shape
    tm = min(MLP_TILE, n)
    full = lambda w: pl.BlockSpec(w.shape, lambda i: (0,) * w.ndim)
    tok = lambda w: pl.BlockSpec((tm, w), lambda i: (i, 0))
    return pl.pallas_call(
        _mix_mlp_kernel,
        grid=(n // tm,),
        in_specs=[tok(a.shape[1]), tok(c.shape[1]), tok(d), full(woa), full(woc), full(g1), full(g2),
                  full(wup), full(wdn), full(g3)],
        out_specs=tok(d),
        out_shape=jax.ShapeDtypeStruct((n, d), F32),
        compiler_params=pltpu.CompilerParams(dimension_semantics=("arbitrary",), vmem_limit_bytes=VMEM_LIMIT),
        name="mix_mlp",
    )(a, c, x, woa, woc, g1, g2, wup, wdn, g3)


def _rope_tables(pos):
    rot = HEAD_DIM // 4
    half = rot // 2
    inv_freq = jnp.power(ROPE_THETA, -jnp.arange(half, dtype=F32) * (2.0 / rot))
    dim = np.arange(LANES) % HEAD_DIM
    ang = pos.astype(F32)[:, None] * inv_freq[dim % half][None, :]
    cos, sin = jnp.cos(ang), jnp.sin(ang)
    rc = jnp.where((dim < rot)[None, :], cos, 1.0)
    ra = jnp.where(((dim >= half) & (dim < rot))[None, :], sin, 0.0)
    rb = jnp.where((dim < half)[None, :], -sin, 0.0)
    return rc, ra, rb


def _prep_w_in(w):
    d = w.shape[0]
    sizes = (ATTN_WIDTH, N_KV_HEADS * HEAD_DIM, N_KV_HEADS * HEAD_DIM, IDX_HEADS * IDX_DIM, IDX_DIM, IDX_HEADS)
    offs = np.cumsum((0,) + sizes)
    wq, wk, wv, wqi, wki, wwi = [w[:, offs[i]:offs[i + 1]] for i in range(6)]
    wu = w[:, offs[6]:]
    zpad = jnp.zeros((d, HEAD_DIM), w.dtype)
    qcols = []
    for hh in range(ATTN_HEADS):
        piece = wq[:, hh * HEAD_DIM:(hh + 1) * HEAD_DIM]
        qcols += [piece, zpad] if hh // KV_GROUP == 0 else [zpad, piece]
    wm = jnp.concatenate(qcols + [wk, wv, wu], axis=1).astype(BF16)
    widx = jnp.concatenate([wqi, wki, wwi, jnp.zeros((d, LANES - IDX_DIM - IDX_HEADS), w.dtype)], axis=1)
    wih, wil = _split(widx)
    return wm, wih, wil


def kernel(x_prompt, x_sample, cache_k, cache_v, cache_idx_k, state_conv, page_table, norm_mix_pre, w_in,
           conv_dw_w, conv_dw_b, conv_ln_g, conv_ln_b, w_out, norm_mix_post, norm_mlp_pre, w_up, w_down,
           norm_mlp_post):
    depth = w_in.shape[0]
    b, t, d = x_prompt.shape
    db, ds, _ = x_sample.shape
    n_pool, page = cache_k.shape[1], cache_k.shape[2]
    past_len = page_table.shape[1] * page
    topk_p = min(INDEX_TOPK, t // 4)
    topk_s = min(INDEX_TOPK, (past_len + ds) // 4)
    hist = CONV_WIDTH - 1

    rcp, rap, rbp = _rope_tables(jnp.arange(t, dtype=I32))
    rcs, ras, rbs = [jnp.tile(a, (db, 1)) for a in _rope_tables(past_len + jnp.arange(ds, dtype=I32))]

    xp = x_prompt
    xs = x_sample.reshape(db * ds, d)
    outs = [[] for _ in range(8)]
    row = lambda a: a.reshape(1, -1)
    for l in range(depth):
        wm, wih, wil = _prep_w_in(w_in[l])
        conv_args = (conv_dw_w[l], row(conv_dw_b[l]), row(conv_ln_g[l]), row(conv_ln_b[l]))
        woa = w_out[l, :ATTN_WIDTH].astype(BF16)
        woc = w_out[l, ATTN_WIDTH:].astype(BF16)
        mlp_args = (woa, woc, row(norm_mix_post[l]), row(norm_mlp_pre[l]), w_up[l].astype(BF16),
                    w_down[l].astype(BF16), row(norm_mlp_post[l]))

        (qt, kt32, vt32, kb, vto, kit32, kib, qit, wit, co, ctail) = _inproj_prompt(
            xp, row(norm_mix_pre[l]), wm, wih, wil, rcp, rap, rbp, *conv_args)
        ao = _attn_prompt(qt, qit, wit, kib, kb, vto, topk_p)
        xp = _mix_mlp(ao.reshape(b * t, -1), co.reshape(b * t, -1), xp.reshape(b * t, d), *mlp_args).reshape(b, t, d)
        unt = lambda a: a.reshape(b, N_KV_HEADS, HEAD_DIM, t).transpose(0, 3, 1, 2)
        outs[0].append(unt(kt32))
        outs[1].append(unt(vt32))
        outs[2].append(kit32.transpose(0, 2, 1))
        outs[3].append(ctail[:, HALO - hist:])

        pool_kt = cache_k[l].transpose(0, 2, 3, 1).reshape(n_pool, N_KV_HEADS * HEAD_DIM, page)
        pool_vt = cache_v[l].transpose(0, 2, 3, 1).reshape(n_pool, N_KV_HEADS * HEAD_DIM, page)
        pool_ikt = cache_idx_k[l].transpose(0, 2, 1)
        (qpad, k32, v32, kb, vb, ki32, kih, kil, qih, qil, wi, co, cnew) = _inproj_sample(
            xs, row(norm_mix_pre[l]), wm, wih, wil, rcs, ras, rbs, *conv_args, state_conv[l].transpose(1, 0, 2))
        ao = _attn_sample(qpad, qih, qil, wi, kih, kil, kb, vb, pool_kt, pool_vt, pool_ikt,
                          page_table, topk_s, db)
        xs = _mix_mlp(ao, co, xs, *mlp_args)
        outs[4].append(k32.reshape(db, ds, N_KV_HEADS, HEAD_DIM))
        outs[5].append(v32.reshape(db, ds, N_KV_HEADS, HEAD_DIM))
        outs[6].append(ki32.reshape(db, ds, IDX_DIM))
        outs[7].append(cnew.transpose(1, 0, 2))

    st = [jnp.stack(o, 0) for o in outs]
    return (xp, xs.reshape(db, ds, d), st[0], st[1], st[2], st[3], st[4], st[5], st[6], st[7])
```
